```python
import math
import jax
import jax.numpy as jnp
from jax import lax
import numpy as np

D_MODEL = 1024
BATCH = 4
SEQ = 4096
DEPTH = 4
DEC_BATCH = 128
DEC_SEQ = 8
PAST_LEN = 8192
PAGE_SIZE = 128

D_FF = 2816
SSM_EXPAND = 2
D_INNER = SSM_EXPAND * D_MODEL
SSM_HEAD_DIM = 64
SSM_HEADS = D_INNER // SSM_HEAD_DIM
SSM_GROUPS = 4
SSM_STATE = 128
D_CONV = 4
CONV_DIM = D_INNER + 2 * SSM_GROUPS * SSM_STATE
SSD_CHUNK = 128
MLA_HEADS = 16
Q_LORA = 512
KV_LORA = 256
QK_NOPE = 64
QK_ROPE = 32
V_HEAD = 64
ROPE_BASE = 10000.0
Q_BLOCK = 128
ATTN_SCALE = (QK_NOPE + QK_ROPE) ** -0.5
PLE_DIM = 256
EPS = 1e-6
OFF_Z = D_INNER
OFF_XBC = OFF_Z + CONV_DIM
OFF_DT = OFF_XBC + SSM_HEADS
OFF_Q = OFF_DT + Q_LORA
OFF_KV = OFF_Q + KV_LORA + QK_ROPE
IN_DIM = OFF_KV + 2 * D_MODEL

kernel_name = "gated_ssd_mla_macaron_ple_step"


def rms_norm(x, g):
    x32 = x.astype(jnp.float32)
    y = x32 * lax.rsqrt(jnp.mean(x32 * x32, axis=-1, keepdims=True) + EPS)
    return (y * g.astype(jnp.float32)).astype(x.dtype)


def swiglu_half_step(x, g, w_gu, w_down):
    a, b = jnp.split(rms_norm(x, g) @ w_gu, 2, axis=-1)
    return x + 0.5 * ((jax.nn.silu(a) * b) @ w_down)


def apply_rope(x, pos):
    half = QK_ROPE // 2
    inv = ROPE_BASE ** (-jnp.arange(half, dtype=jnp.float32) / half)
    ang = pos.astype(jnp.float32)[:, None] * inv[None, :]
    shape = (1, ang.shape[0]) + (1,) * (x.ndim - 3) + (half,)
    cos = jnp.cos(ang).reshape(shape)
    sin = jnp.sin(ang).reshape(shape)
    x32 = x.astype(jnp.float32)
    x1, x2 = x32[..., :half], x32[..., half:]
    return jnp.concatenate([x1 * cos - x2 * sin, x2 * cos + x1 * sin], axis=-1).astype(x.dtype)


def segsum(a):
    T = a.shape[-1]
    idx = jnp.arange(T)
    xr = jnp.broadcast_to(a[..., :, None], a.shape + (T,))
    xr = jnp.where(idx[:, None] > idx[None, :], xr, 0.0)
    s = jnp.cumsum(xr, axis=-2)
    return jnp.where(idx[:, None] >= idx[None, :], s, -jnp.inf)


def ssd_chunked(x, dt, a_head, b, c, h0):
    f32 = jnp.float32
    bs, L, H, P = x.shape
    T = SSD_CHUNK if L % SSD_CHUNK == 0 else L
    nc = L // T
    R = H // SSM_GROUPS
    xd = (x.astype(f32) * dt[..., None]).reshape(bs, nc, T, SSM_GROUPS, R, P)
    bc = b.astype(f32).reshape(bs, nc, T, SSM_GROUPS, SSM_STATE)
    cc = c.astype(f32).reshape(bs, nc, T, SSM_GROUPS, SSM_STATE)
    a = (dt * a_head).reshape(bs, nc, T, SSM_GROUPS, R).transpose(0, 3, 4, 1, 2)
    a_cum = jnp.cumsum(a, axis=-1)
    l_mat = jnp.exp(segsum(a))
    y_diag = jnp.einsum("bclgn,bcsgn,bgrcls,bcsgrp->bclgrp", cc, bc, l_mat, xd)
    decay_states = jnp.exp(a_cum[..., -1:] - a_cum)
    states = jnp.einsum("bclgn,bgrcl,bclgrp->bcgrpn", bc, decay_states, xd)
    h0g = h0.astype(f32).reshape(bs, 1, SSM_GROUPS, R, P, SSM_STATE)
    states = jnp.concatenate([h0g, states], axis=1)
    chunk_a = jnp.pad(a_cum[..., -1], [(0, 0), (0, 0), (0, 0), (1, 0)])
    decay_chunk = jnp.exp(segsum(chunk_a))
    states = jnp.einsum("bgrzc,bcgrpn->bzgrpn", decay_chunk, states)
    prev_states, final = states[:, :-1], states[:, -1]
    y_off = jnp.einsum("bclgn,bcgrpn,bgrcl->bclgrp", cc, prev_states, jnp.exp(a_cum))
    y = (y_diag + y_off).reshape(bs, L, H, P)
    return y, final.reshape(bs, H, P, SSM_STATE)


def ssm_branch(z, xbc, dt_raw, conv_prev, ssm_prev, lw):
    f32 = jnp.float32
    bs, L, _ = xbc.shape
    xpad = jnp.concatenate([conv_prev.astype(xbc.dtype), xbc], axis=1)
    w = lw["conv_w"]
    acc = xpad[:, 0:L] * w[0]
    for k in range(1, D_CONV):
        acc = acc + xpad[:, k:k + L] * w[k]
    conv = jax.nn.silu(acc + lw["conv_b"])
    new_conv = xpad[:, xpad.shape[1] - (D_CONV - 1):]
    xs, bm, cm = jnp.split(conv, [D_INNER, D_INNER + SSM_GROUPS * SSM_STATE], axis=-1)
    xs = xs.reshape(bs, L, SSM_HEADS, SSM_HEAD_DIM)
    bm = bm.reshape(bs, L, SSM_GROUPS, SSM_STATE)
    cm = cm.reshape(bs, L, SSM_GROUPS, SSM_STATE)
    dt = jax.nn.softplus(dt_raw.astype(f32) + lw["dt_bias"].astype(f32))
    a_head = -jnp.exp(lw["a_log"].astype(f32))
    y, h_final = ssd_chunked(xs, dt, a_head, bm, cm, ssm_prev)
    y = y + lw["d_skip"].astype(f32)[:, None] * xs.astype(f32)
    y = y.reshape(bs, L, D_INNER) * jax.nn.silu(z.astype(f32))
    yg = y.reshape(bs, L, SSM_GROUPS, D_INNER // SSM_GROUPS)
    yg = yg * lax.rsqrt(jnp.mean(yg * yg, axis=-1, keepdims=True) + EPS)
    y = (yg.reshape(bs, L, D_INNER) * lw["ssm_norm"].astype(f32)).astype(z.dtype)
    return y, new_conv, h_final


def mla_project(q_lat, kv_a, pos, lw):
    bs, L, _ = q_lat.shape
    q = (rms_norm(q_lat, lw["q_norm"]) @ lw["w_qb"]).reshape(bs, L, MLA_HEADS, QK_NOPE + QK_ROPE)
    q_nope = q[..., :QK_NOPE]
    q_rope = apply_rope(q[..., QK_NOPE:], pos)
    c_kv = rms_norm(kv_a[..., :KV_LORA], lw["kv_norm"])
    k_rope = apply_rope(kv_a[..., KV_LORA:], pos)
    return q_nope, q_rope, c_kv, k_rope


def mla_prefill(q_nope, q_rope, c_kv, k_rope, lw):
    f32 = jnp.float32
    bs, L = c_kv.shape[0], c_kv.shape[1]
    k_nope = jnp.einsum("btc,chd->bthd", c_kv, lw["w_uk"])
    v = jnp.einsum("btc,chv->bthv", c_kv, lw["w_uv"])
    qb = Q_BLOCK if L % Q_BLOCK == 0 else L
    nb = L // qb
    qn = q_nope.reshape(bs, nb, qb, MLA_HEADS, QK_NOPE).transpose(1, 0, 2, 3, 4)
    qr = q_rope.reshape(bs, nb, qb, MLA_HEADS, QK_ROPE).transpose(1, 0, 2, 3, 4)
    kpos = jnp.arange(L)

    def one_block(args):
        qn_b, qr_b, blk = args
        s = (jnp.einsum("bqhd,bthd->bhqt", qn_b, k_nope, preferred_element_type=f32)
             + jnp.einsum("bqhr,btr->bhqt", qr_b, k_rope, preferred_element_type=f32)) * ATTN_SCALE
        qpos = blk * qb + jnp.arange(qb)
        s = jnp.where(kpos[None, :] <= qpos[:, None], s, -jnp.inf)
        p = jax.nn.softmax(s, axis=-1).astype(v.dtype)
        return jnp.einsum("bhqt,bthv->bqhv", p, v)

    o = lax.map(one_block, (qn, qr, jnp.arange(nb)))
    return o.transpose(1, 0, 2, 3, 4).reshape(bs, L, MLA_HEADS * V_HEAD)


def mla_decode(q_nope, q_rope, c_kv, k_rope, cache_kv_l, cache_kr_l, page_table, lw):
    f32 = jnp.float32
    bs, L = c_kv.shape[0], c_kv.shape[1]
    c_past = cache_kv_l[page_table].reshape(bs, -1, KV_LORA)
    r_past = cache_kr_l[page_table].reshape(bs, -1, QK_ROPE)
    n_past = c_past.shape[1]
    q_abs = jnp.einsum("blhd,chd->blhc", q_nope, lw["w_uk"])
    s_past = (jnp.einsum("blhc,btc->bhlt", q_abs, c_past, preferred_element_type=f32)
              + jnp.einsum("blhr,btr->bhlt", q_rope, r_past, preferred_element_type=f32))
    s_new = (jnp.einsum("blhc,btc->bhlt", q_abs, c_kv, preferred_element_type=f32)
             + jnp.einsum("blhr,btr->bhlt", q_rope, k_rope, preferred_element_type=f32))
    idx = jnp.arange(L)
    s_new = jnp.where(idx[None, :] <= idx[:, None], s_new, -jnp.inf)
    p = jax.nn.softmax(jnp.concatenate([s_past, s_new], axis=-1) * ATTN_SCALE, axis=-1).astype(c_kv.dtype)
    o_lat = (jnp.einsum("bhlt,btc->blhc", p[..., :n_past], c_past)
             + jnp.einsum("bhlt,btc->blhc", p[..., n_past:], c_kv))
    o = jnp.einsum("blhc,chv->blhv", o_lat, lw["w_uv"])
    return o.reshape(bs, L, MLA_HEADS * V_HEAD)


def trunk_layer(x, p_l, pos, conv_prev, ssm_prev, lw, paged):
    x = swiglu_half_step(x, lw["ffn1_norm"], lw["ffn1_w_gu"], lw["ffn1_w_down"])
    u = rms_norm(x, lw["mix_norm"])
    proj = u @ lw["w_in"]
    z, xbc, dt_raw, q_lat, kv_a, gates = jnp.split(proj, [OFF_Z, OFF_XBC, OFF_DT, OFF_Q, OFF_KV], axis=-1)
    y_ssm, new_conv, new_ssm = ssm_branch(z, xbc, dt_raw, conv_prev, ssm_prev, lw)
    q_nope, q_rope, c_kv, k_rope = mla_project(q_lat, kv_a, pos, lw)
    if paged is None:
        y_mla = mla_prefill(q_nope, q_rope, c_kv, k_rope, lw)
    else:
        y_mla = mla_decode(q_nope, q_rope, c_kv, k_rope, paged[0], paged[1], paged[2], lw)
    g_ssm, g_mla = jnp.split(jax.nn.sigmoid(gates), 2, axis=-1)
    merged = g_ssm * (y_ssm @ lw["w_br_ssm"]) + g_mla * (y_mla @ lw["w_br_mla"])
    x = x + merged @ lw["w_out"]
    x = swiglu_half_step(x, lw["ffn2_norm"], lw["ffn2_w_gu"], lw["ffn2_w_down"])
    ple_gate = jax.nn.sigmoid(rms_norm(x, lw["ple_norm"]) @ lw["w_ple_gate"])
    x = x + ple_gate * (p_l.astype(x.dtype) @ lw["w_ple_proj"])
    return x, c_kv, k_rope, new_conv, new_ssm


def setup_inputs(seed: int = 0) -> dict:
    key = jax.random.key(seed)
    ks = iter(jax.random.split(key, 64))
    f32 = jnp.float32
    n_pages = PAST_LEN // PAGE_SIZE
    n_pool = (DEC_BATCH * n_pages * 5) // 4

    def normal(shape, scale):
        return jax.random.normal(next(ks), shape, f32) * scale

    def dense(shape, fan_in):
        return normal(shape, fan_in ** -0.5)

    def gain(shape):
        return 1.0 + normal(shape, 0.05)

    x_prompt = normal((BATCH, SEQ, D_MODEL), 1.0)
    x_sample = normal((DEC_BATCH, DEC_SEQ, D_MODEL), 1.0)
    cache_kv = normal((DEPTH, n_pool, PAGE_SIZE, KV_LORA), 1.0)
    cache_kr = normal((DEPTH, n_pool, PAGE_SIZE, QK_ROPE), 1.0)
    state_conv = normal((DEPTH, DEC_BATCH, D_CONV - 1, CONV_DIM), 1.0)
    state_ssm = normal((DEPTH, DEC_BATCH, SSM_HEADS, SSM_HEAD_DIM, SSM_STATE), 0.5)
    perm = jax.random.permutation(next(ks), n_pool)
    page_table = perm[:DEC_BATCH * n_pages].reshape(DEC_BATCH, n_pages).astype(jnp.int32)
    p_prompt = normal((DEPTH, BATCH, SEQ, PLE_DIM), 1.0)
    p_sample = normal((DEPTH, DEC_BATCH, DEC_SEQ, PLE_DIM), 1.0)

    u_dt = jax.random.uniform(next(ks), (DEPTH, SSM_HEADS), f32)
    dt0 = jnp.exp(u_dt * (math.log(0.1) - math.log(0.001)) + math.log(0.001))
    dt_bias = dt0 + jnp.log(-jnp.expm1(-dt0))
    a_log = jnp.log(jax.random.uniform(next(ks), (DEPTH, SSM_HEADS), f32, minval=1.0, maxval=16.0))

    return {
        "x_prompt": x_prompt,
        "x_sample": x_sample,
        "cache_kv": cache_kv,
        "cache_kr": cache_kr,
        "state_conv": state_conv,
        "state_ssm": state_ssm,
        "page_table": page_table,
        "p_prompt": p_prompt,
        "p_sample": p_sample,
        "ffn1_norm": gain((DEPTH, D_MODEL)),
        "ffn1_w_gu": dense((DEPTH, D_MODEL, 2 * D_FF), D_MODEL),
        "ffn1_w_down": dense((DEPTH, D_FF, D_MODEL), D_FF),
        "mix_norm": gain((DEPTH, D_MODEL)),
        "w_in": dense((DEPTH, D_MODEL, IN_DIM), D_MODEL),
        "conv_w": dense((DEPTH, D_CONV, CONV_DIM), D_CONV),
        "conv_b": normal((DEPTH, CONV_DIM), 0.01),
        "dt_bias": dt_bias,
        "a_log": a_log,
        "d_skip": gain((DEPTH, SSM_HEADS)),
        "ssm_norm": gain((DEPTH, D_INNER)),
        "q_norm": gain((DEPTH, Q_LORA)),
        "w_qb": dense((DEPTH, Q_LORA, MLA_HEADS * (QK_NOPE + QK_ROPE)), Q_LORA),
        "kv_norm": gain((DEPTH, KV_LORA)),
        "w_uk": dense((DEPTH, KV_LORA, MLA_HEADS, QK_NOPE), KV_LORA),
        "w_uv": dense((DEPTH, KV_LORA, MLA_HEADS, V_HEAD), KV_LORA),
        "w_br_ssm": dense((DEPTH, D_INNER, D_MODEL), D_INNER),
        "w_br_mla": dense((DEPTH, MLA_HEADS * V_HEAD, D_MODEL), MLA_HEADS * V_HEAD),
        "w_out": dense((DEPTH, D_MODEL, D_MODEL), D_MODEL),
        "ffn2_norm": gain((DEPTH, D_MODEL)),
        "ffn2_w_gu": dense((DEPTH, D_MODEL, 2 * D_FF), D_MODEL),
        "ffn2_w_down": dense((DEPTH, D_FF, D_MODEL), D_FF),
        "ple_norm": gain((DEPTH, D_MODEL)),
        "w_ple_gate": dense((DEPTH, D_MODEL, D_MODEL), D_MODEL),
        "w_ple_proj": dense((DEPTH, PLE_DIM, D_MODEL), PLE_DIM),
        "final_norm": gain((D_MODEL,)),
    }


def reference(x_prompt, x_sample, cache_kv, cache_kr, state_conv, state_ssm, page_table, p_prompt, p_sample,
              ffn1_norm, ffn1_w_gu, ffn1_w_down, mix_norm, w_in, conv_w, conv_b, dt_bias, a_log, d_skip,
              ssm_norm, q_norm, w_qb, kv_norm, w_uk, w_uv, w_br_ssm, w_br_mla, w_out,
              ffn2_norm, ffn2_w_gu, ffn2_w_down, ple_norm, w_ple_gate, w_ple_proj, final_norm):
    bp, sp, _ = x_prompt.shape
    past_len = page_table.shape[1] * cache_kv.shape[2]
    pos_p = jnp.arange(sp)
    pos_s = past_len + jnp.arange(x_sample.shape[1])
    conv0 = jnp.zeros((bp, D_CONV - 1, CONV_DIM), x_prompt.dtype)
    ssm0 = jnp.zeros((bp, SSM_HEADS, SSM_HEAD_DIM, SSM_STATE), jnp.float32)

    hp, hs = x_prompt, x_sample
    kv_p, kr_p, cv_p, ss_p = [], [], [], []
    kv_s, kr_s, cv_s, ss_s = [], [], [], []
    for i in range(DEPTH):
        lw = {
            "ffn1_norm": ffn1_norm[i], "ffn1_w_gu": ffn1_w_gu[i], "ffn1_w_down": ffn1_w_down[i],
            "mix_norm": mix_norm[i], "w_in": w_in[i], "conv_w": conv_w[i], "conv_b": conv_b[i],
            "dt_bias": dt_bias[i], "a_log": a_log[i], "d_skip": d_skip[i], "ssm_norm": ssm_norm[i],
            "q_norm": q_norm[i], "w_qb": w_qb[i], "kv_norm": kv_norm[i], "w_uk": w_uk[i], "w_uv": w_uv[i],
            "w_br_ssm": w_br_ssm[i], "w_br_mla": w_br_mla[i], "w_out": w_out[i],
            "ffn2_norm": ffn2_norm[i], "ffn2_w_gu": ffn2_w_gu[i], "ffn2_w_down": ffn2_w_down[i],
            "ple_norm": ple_norm[i], "w_ple_gate": w_ple_gate[i], "w_ple_proj": w_ple_proj[i],
        }
        hp, c_p, r_p, conv_p, ssm_p = trunk_layer(hp, p_prompt[i], pos_p, conv0, ssm0, lw, None)
        hs, c_s, r_s, conv_s, ssm_s = trunk_layer(hs, p_sample[i], pos_s, state_conv[i], state_ssm[i], lw,
                                                  (cache_kv[i], cache_kr[i], page_table))
        kv_p.append(c_p); kr_p.append(r_p); cv_p.append(conv_p); ss_p.append(ssm_p)
        kv_s.append(c_s); kr_s.append(r_s); cv_s.append(conv_s); ss_s.append(ssm_s)

    y_prompt = rms_norm(hp, final_norm)
    y_sample = rms_norm(hs, final_norm)
    new_kv_prompt = jnp.stack(kv_p)
    new_kr_prompt = jnp.stack(kr_p)
    new_conv_prompt = jnp.stack(cv_p)
    new_ssm_prompt = jnp.stack(ss_p)
    new_kv_sample = jnp.stack(kv_s)
    new_kr_sample = jnp.stack(kr_s)
    new_conv_sample = jnp.stack(cv_s)
    new_ssm_sample = jnp.stack(ss_s)
    return (y_prompt, y_sample, new_kv_prompt, new_kr_prompt, new_conv_prompt, new_ssm_prompt,
            new_kv_sample, new_kr_sample, new_conv_sample, new_ssm_sample)
```

```python
import functools
import math

import jax
import jax.numpy as jnp
from jax import lax
from jax.experimental import pallas as pl
from jax.experimental.pallas import tpu as pltpu

F32 = jnp.float32
BF16 = jnp.bfloat16

D_MODEL = 1024
D_FF = 2816
D_INNER = 2048
SSM_HEAD_DIM = 64
SSM_HEADS = 32
SSM_GROUPS = 4
HEADS_PER_GROUP = SSM_HEADS // SSM_GROUPS
SSM_STATE = 128
GROUP_CH = D_INNER // SSM_GROUPS
D_CONV = 4
CONV_DIM = D_INNER + 2 * SSM_GROUPS * SSM_STATE
SSD_CHUNK = 128
MLA_HEADS = 16
Q_LORA = 512
KV_LORA = 256
QK_NOPE = 64
QK_ROPE = 32
V_HEAD = 64
ROPE_BASE = 10000.0
ATTN_SCALE = (QK_NOPE + QK_ROPE) ** -0.5
PLE_DIM = 256
EPS = 1e-6
LANES = 128
HEAD_PAD = 128
NEG_BIG = -1e30

C_Z, C_XS, C_BC, C_GS, C_GM, C_Q = 0, 2048, 4096, 5120, 6144, 7168
MAIN_DIM = 7680
C_DTX = 8192
MAIN_DIM_X = C_DTX + D_INNER
SMALL_DIM = 384
S_DT, S_ROPE = 256, 320

VMEM_LIMIT = 56 * 1024 * 1024


def _cparams(sem):
    return pltpu.CompilerParams(dimension_semantics=sem, vmem_limit_bytes=VMEM_LIMIT)


def _pick(n, prefs):
    for p in prefs:
        if n % p == 0:
            return p
    return n


def _dot(a, b):
    return jnp.dot(a, b, preferred_element_type=F32)


def _dot_nt(a, b):
    return lax.dot_general(a, b, (((1,), (1,)), ((), ())), preferred_element_type=F32)


def _silu(x):
    return x * jax.nn.sigmoid(x)


def _softplus(x):
    return jnp.maximum(x, 0.0) + jnp.log1p(jnp.exp(-jnp.abs(x)))


def _rms(x, g):
    return x * lax.rsqrt(jnp.mean(x * x, axis=-1, keepdims=True) + EPS) * g


def _nm_kernel(x_ref, g_ref, w_ref, o_ref, u_sc, *, norm):
    @pl.when(pl.program_id(1) == 0)
    def _():
        x = x_ref[...].astype(F32)
        if norm:
            x = _rms(x, g_ref[...])
        u_sc[...] = x.astype(BF16)

    o_ref[...] = _dot(u_sc[...], w_ref[...]).astype(o_ref.dtype)


def norm_matmul(x, g, w, out_dtype, *, norm=True, col_block=0, tm_prefs=(1024, 512, 256, 128)):
    n = x.shape[0]
    k, nout = w.shape
    tm = _pick(n, tm_prefs)
    tn = _pick(nout, (1280, 1024, 512, 384, 256, 128))
    if not norm:
        g = jnp.ones((1, k), F32)
    return pl.pallas_call(
        functools.partial(_nm_kernel, norm=norm),
        grid=(n // tm, nout // tn),
        in_specs=[
            pl.BlockSpec((tm, k), lambda i, j: (i, col_block)),
            pl.BlockSpec((1, k), lambda i, j: (0, 0)),
            pl.BlockSpec((k, tn), lambda i, j: (0, j)),
        ],
        out_specs=pl.BlockSpec((tm, tn), lambda i, j: (i, j)),
        out_shape=jax.ShapeDtypeStruct((n, nout), out_dtype),
        scratch_shapes=[pltpu.VMEM((tm, k), BF16)],
        compiler_params=_cparams(("parallel", "arbitrary")),
        name="norm_matmul",
    )(x, g, w)


def _ffn_kernel(x_ref, g_ref, wg_ref, wu_ref, wd_ref, o_ref, u_sc, acc_sc):
    j = pl.program_id(1)

    @pl.when(j == 0)
    def _():
        u_sc[...] = _rms(x_ref[...], g_ref[...]).astype(BF16)
        acc_sc[...] = jnp.zeros_like(acc_sc)

    u = u_sc[...]
    a = _dot(u, wg_ref[...])
    b = _dot(u, wu_ref[...])
    h = (_silu(a) * b).astype(BF16)
    acc_sc[...] += _dot(h, wd_ref[...])

    @pl.when(j == pl.num_programs(1) - 1)
    def _():
        o_ref[...] = x_ref[...] + 0.5 * acc_sc[...]


def ffn_half_step(x, g, w_gu, w_down):
    n, d = x.shape
    dff = w_down.shape[0]
    tm = _pick(n, (1024, 512, 256, 128))
    tf = _pick(dff, (256, 128))
    nf = dff // tf
    return pl.pallas_call(
        _ffn_kernel,
        grid=(n // tm, nf),
        in_specs=[
            pl.BlockSpec((tm, d), lambda i, j: (i, 0)),
            pl.BlockSpec((1, d), lambda i, j: (0, 0)),
            pl.BlockSpec((d, tf), lambda i, j: (0, j)),
            pl.BlockSpec((d, tf), lambda i, j: (0, j + nf)),
            pl.BlockSpec((tf, d), lambda i, j: (j, 0)),
        ],
        out_specs=pl.BlockSpec((tm, d), lambda i, j: (i, 0)),
        out_shape=jax.ShapeDtypeStruct((n, d), F32),
        scratch_shapes=[pltpu.VMEM((tm, d), BF16), pltpu.VMEM((tm, d), F32)],
        compiler_params=_cparams(("parallel", "arbitrary")),
        name="ffn_half_step",
    )(x, g, w_gu, w_gu, w_down)


def _causal_conv(xpad_sc, rows, w_ref, b_ref):
    w = w_ref[...]
    acc = xpad_sc[5:5 + rows, :] * w[0:1, :]
    for k in range(1, D_CONV):
        acc = acc + xpad_sc[5 + k:5 + k + rows, :] * w[k:k + 1, :]
    return _silu(acc + b_ref[...])


def _gate_and_group_norm(y, z, norm_w):
    y = y * _silu(z)
    parts = []
    for g in range(SSM_GROUPS):
        yg = y[:, g * GROUP_CH:(g + 1) * GROUP_CH]
        parts.append(yg * lax.rsqrt(jnp.mean(yg * yg, axis=-1, keepdims=True) + EPS))
    return jnp.concatenate(parts, axis=1) * norm_w


def _cumsum_rows(a):
    rows = a.shape[0]
    row = lax.broadcasted_iota(jnp.int32, a.shape, 0)
    s = 1
    while s < rows:
        a = a + jnp.where(row >= s, pltpu.roll(a, s, axis=0), 0.0)
        s *= 2
    return a


def _ssm_prefill_kernel(z_ref, xs_ref, bc_ref, dt_ref, cprev_ref, sprev_ref, convw_ref, convb_ref,
                        dtb_ref, aneg_ref, dskip_ref, norm_ref,
                        y_ref, nconv_ref, nssm_ref, xpad_sc, h_sc):
    c = pl.program_id(1)
    t = SSD_CHUNK
    hp = SSM_HEAD_DIM

    @pl.when(c == 0)
    def _():
        xpad_sc[5:8, :] = cprev_ref[0]
        h_sc[...] = sprev_ref[0].reshape(D_INNER, SSM_STATE)

    xpad_sc[8:8 + t, 0:D_INNER] = xs_ref[...]
    xpad_sc[8:8 + t, D_INNER:CONV_DIM] = bc_ref[...]
    conv = _causal_conv(xpad_sc, t, convw_ref, convb_ref)
    tail = xpad_sc[8 + t - 3:8 + t, :]
    xpad_sc[5:8, :] = tail

    xs = conv[:, :D_INNER]
    gs = SSM_GROUPS * SSM_STATE
    bm = conv[:, D_INNER:D_INNER + gs]
    cm = conv[:, D_INNER + gs:]

    dt = _softplus(dt_ref[...] + dtb_ref[...])
    acum = _cumsum_rows(dt * aneg_ref[...])
    alast = acum[t - 1:t, :]
    acum_t = acum.T
    dt_t = dt.T
    wst_t = (dt * jnp.exp(alast - acum)).T
    xs_t = xs.T

    tri = (lax.broadcasted_iota(jnp.int32, (t, t), 0) >= lax.broadcasted_iota(jnp.int32, (t, t), 1))
    low_half = lax.broadcasted_iota(jnp.int32, (t, 2 * hp), 1) < hp

    y_parts = []
    for g in range(SSM_GROUPS):
        bg = bm[:, g * SSM_STATE:(g + 1) * SSM_STATE]
        cg = cm[:, g * SSM_STATE:(g + 1) * SSM_STATE]
        cb = _dot_nt(cg.astype(BF16), bg.astype(BF16))
        for pair in range(HEADS_PER_GROUP // 2):
            h0 = g * HEADS_PER_GROUP + 2 * pair
            lhs = []
            for h in (h0, h0 + 1):
                colb = jnp.broadcast_to(acum[:, h:h + 1], (t, t))
                decay = jnp.where(tri, jnp.exp(colb - acum_t[h:h + 1, :]), 0.0)
                m = cb * decay * dt_t[h:h + 1, :]
                ce = cg * jnp.exp(colb)
                lhs.append(jnp.concatenate([m, ce], axis=1))
            lhs = jnp.concatenate(lhs, axis=0).astype(BF16)
            r0 = h0 * hp
            rhs_t = jnp.concatenate([xs_t[r0:r0 + 2 * hp, :], h_sc[r0:r0 + 2 * hp, :]],
                                    axis=1).astype(BF16)
            out = _dot_nt(lhs, rhs_t)
            y_parts.append(jnp.where(low_half, out[:t, :], out[t:, :]))
    y = jnp.concatenate(y_parts, axis=1) + dskip_ref[...] * xs
    y_ref[...] = _gate_and_group_norm(y, z_ref[...], norm_ref[...]).astype(y_ref.dtype)

    elast = jnp.exp(alast)
    for g in range(SSM_GROUPS):
        bg = bm[:, g * SSM_STATE:(g + 1) * SSM_STATE].astype(BF16)
        xw = []
        for hh in range(HEADS_PER_GROUP):
            h = g * HEADS_PER_GROUP + hh
            xw.append(xs_t[h * hp:(h + 1) * hp, :] * wst_t[h:h + 1, :])
        dh = _dot(jnp.concatenate(xw, axis=0).astype(BF16), bg)
        for hh in range(HEADS_PER_GROUP):
            h = g * HEADS_PER_GROUP + hh
            rows = slice(h * hp, (h + 1) * hp)
            h_sc[rows, :] = h_sc[rows, :] * elast[:, h:h + 1] + dh[hh * hp:(hh + 1) * hp, :]

    @pl.when(c == pl.num_programs(1) - 1)
    def _():
        nconv_ref[0] = tail
        nssm_ref[0] = h_sc[...].reshape(SSM_HEADS, SSM_HEAD_DIM, SSM_STATE)


def ssm_prefill(proj, small, conv_prev, ssm_prev, w, bs, seq):
    t = SSD_CHUNK
    nc = seq // t
    n = bs * seq
    row = lambda b, c: b * nc + c
    vec = lambda width: pl.BlockSpec((1, width), lambda b, c: (0, 0))
    return pl.pallas_call(
        _ssm_prefill_kernel,
        grid=(bs, nc),
        in_specs=[
            pl.BlockSpec((t, D_INNER), lambda b, c: (row(b, c), C_Z // D_INNER)),
            pl.BlockSpec((t, D_INNER), lambda b, c: (row(b, c), C_XS // D_INNER)),
            pl.BlockSpec((t, 1024), lambda b, c: (row(b, c), C_BC // 1024)),
            pl.BlockSpec((t, LANES), lambda b, c: (row(b, c), S_DT // LANES)),
            pl.BlockSpec((1, D_CONV - 1, CONV_DIM), lambda b, c: (b, 0, 0)),
            pl.BlockSpec((1, SSM_HEADS, SSM_HEAD_DIM, SSM_STATE), lambda b, c: (b, 0, 0, 0)),
            pl.BlockSpec((D_CONV, CONV_DIM), lambda b, c: (0, 0)),
            vec(CONV_DIM), vec(LANES), vec(LANES), vec(D_INNER), vec(D_INNER),
        ],
        out_specs=[
            pl.BlockSpec((t, D_INNER), lambda b, c: (row(b, c), 0)),
            pl.BlockSpec((1, D_CONV - 1, CONV_DIM), lambda b, c: (b, 0, 0)),
            pl.BlockSpec((1, SSM_HEADS, SSM_HEAD_DIM, SSM_STATE), lambda b, c: (b, 0, 0, 0)),
        ],
        out_shape=[
            jax.ShapeDtypeStruct((n, D_INNER), F32),
            jax.ShapeDtypeStruct((bs, D_CONV - 1, CONV_DIM), F32),
            jax.ShapeDtypeStruct((bs, SSM_HEADS, SSM_HEAD_DIM, SSM_STATE), F32),
        ],
        scratch_shapes=[pltpu.VMEM((t + 8, CONV_DIM), F32), pltpu.VMEM((D_INNER, SSM_STATE), F32)],
        compiler_params=_cparams(("parallel", "arbitrary")),
        name="ssm_prefill",
    )(proj, proj, proj, small, conv_prev, ssm_prev, w["conv_w"], w["conv_b"], w["dt_bias_l"],
      w["a_neg_l"], w["d_skip_x"], w["ssm_norm"])


def _ssm_step_kernel(z_ref, xs_ref, bc_ref, dtx_ref, cprev_ref, sprev_ref, convw_ref, convb_ref,
                     dtbx_ref, anegx_ref, dskip_ref, norm_ref,
                     y_ref, nconv_ref, nssm_ref, xpad_sc, xw_sc, bpad_sc):
    seq = xs_ref.shape[0]
    hp = SSM_HEAD_DIM
    gs = SSM_GROUPS * SSM_STATE

    @pl.when(pl.program_id(0) == 0)
    def _():
        xw_sc[...] = jnp.zeros_like(xw_sc)
        bpad_sc[...] = jnp.zeros_like(bpad_sc)

    xpad_sc[5:8, :] = cprev_ref[0]
    xpad_sc[8:8 + seq, 0:D_INNER] = xs_ref[...]
    xpad_sc[8:8 + seq, D_INNER:CONV_DIM] = bc_ref[...]
    conv = _causal_conv(xpad_sc, seq, convw_ref, convb_ref)
    nconv_ref[0] = xpad_sc[8 + seq - 3:8 + seq, :]

    xs = conv[:, :D_INNER]
    bm = conv[:, D_INNER:D_INNER + gs]
    cm = conv[:, D_INNER + gs:]

    dtx = _softplus(dtx_ref[...] + dtbx_ref[...])
    acum = _cumsum_rows(dtx * anegx_ref[...])
    alast = acum[seq - 1:seq, :]
    xd = xs * dtx

    c_all = jnp.concatenate([cm[:, g * SSM_STATE:(g + 1) * SSM_STATE] for g in range(SSM_GROUPS)],
                            axis=0).astype(BF16)
    y_off = []
    for g in range(SSM_GROUPS):
        hg = sprev_ref[0, g * HEADS_PER_GROUP:(g + 1) * HEADS_PER_GROUP].reshape(GROUP_CH, SSM_STATE)
        y_off.append(_dot_nt(c_all, hg.astype(BF16))[g * seq:(g + 1) * seq, :])
    y = jnp.concatenate(y_off, axis=1) * jnp.exp(acum)

    ones = jnp.ones((SSM_STATE, GROUP_CH), BF16)
    cbx = []
    for g in range(SSM_GROUPS):
        bg = bm[:, g * SSM_STATE:(g + 1) * SSM_STATE]
        cg = cm[:, g * SSM_STATE:(g + 1) * SSM_STATE]
        prod = jnp.concatenate([cg * bg[s:s + 1, :] for s in range(seq)], axis=0)
        hi = prod.astype(BF16)
        lo = (prod - hi.astype(F32)).astype(BF16)
        cbx.append(_dot(hi, ones) + _dot(lo, ones))
    row = lax.broadcasted_iota(jnp.int32, (seq, D_INNER), 0)
    for s in range(seq):
        cb_s = jnp.concatenate([cbx[g][s * seq:(s + 1) * seq, :] for g in range(SSM_GROUPS)], axis=1)
        decay = jnp.where(row >= s, jnp.exp(acum - acum[s:s + 1, :]), 0.0)
        y = y + cb_s * decay * xd[s:s + 1, :]

    y = y + dskip_ref[...] * xs
    y_ref[...] = _gate_and_group_norm(y, z_ref[...], norm_ref[...]).astype(y_ref.dtype)

    xw_sc[0:seq, :] = xd * jnp.exp(alast - acum)
    bpad_sc[0:seq, :] = bm
    xw_t = xw_sc[...].T
    elast = jnp.exp(alast)
    for g in range(SSM_GROUPS):
        dh = _dot(xw_t[g * GROUP_CH:(g + 1) * GROUP_CH, :].astype(BF16),
                  bpad_sc[:, g * SSM_STATE:(g + 1) * SSM_STATE].astype(BF16))
        for hh in range(HEADS_PER_GROUP):
            h = g * HEADS_PER_GROUP + hh
            nssm_ref[0, h] = (sprev_ref[0, h] * elast[:, h * hp:h * hp + 1]
                              + dh[hh * hp:(hh + 1) * hp, :])


def ssm_step(proj, conv_prev, ssm_prev, w, bs, seq):
    n = bs * seq
    vec = lambda width: pl.BlockSpec((1, width), lambda b: (0, 0))
    return pl.pallas_call(
        _ssm_step_kernel,
        grid=(bs,),
        in_specs=[
            pl.BlockSpec((seq, D_INNER), lambda b: (b, C_Z // D_INNER)),
            pl.BlockSpec((seq, D_INNER), lambda b: (b, C_XS // D_INNER)),
            pl.BlockSpec((seq, 1024), lambda b: (b, C_BC // 1024)),
            pl.BlockSpec((seq, D_INNER), lambda b: (b, C_DTX // D_INNER)),
            pl.BlockSpec((1, D_CONV - 1, CONV_DIM), lambda b: (b, 0, 0)),
            pl.BlockSpec((1, SSM_HEADS, SSM_HEAD_DIM, SSM_STATE), lambda b: (b, 0, 0, 0)),
            pl.BlockSpec((D_CONV, CONV_DIM), lambda b: (0, 0)),
            vec(CONV_DIM), vec(D_INNER), vec(D_INNER), vec(D_INNER), vec(D_INNER),
        ],
        out_specs=[
            pl.BlockSpec((seq, D_INNER), lambda b: (b, 0)),
            pl.BlockSpec((1, D_CONV - 1, CONV_DIM), lambda b: (b, 0, 0)),
            pl.BlockSpec((1, SSM_HEADS, SSM_HEAD_DIM, SSM_STATE), lambda b: (b, 0, 0, 0)),
        ],
        out_shape=[
            jax.ShapeDtypeStruct((n, D_INNER), F32),
            jax.ShapeDtypeStruct((bs, D_CONV - 1, CONV_DIM), F32),
            jax.ShapeDtypeStruct((bs, SSM_HEADS, SSM_HEAD_DIM, SSM_STATE), F32),
        ],
        scratch_shapes=[pltpu.VMEM((seq + 8, CONV_DIM), F32), pltpu.VMEM((LANES, D_INNER), F32),
                        pltpu.VMEM((LANES, SSM_GROUPS * SSM_STATE), F32)],
        compiler_params=_cparams(("arbitrary",)),
        name="ssm_step",
    )(proj, proj, proj, proj, conv_prev, ssm_prev, w["conv_w"], w["conv_b"], w["dt_bias_x"],
      w["a_neg_x"], w["d_skip_x"], w["ssm_norm"])


def _rope(x, cos, sin_hi, sin_lo):
    return x * cos + pltpu.roll(x, 16, axis=1) * sin_hi + pltpu.roll(x, x.shape[1] - 16, axis=1) * sin_lo


def _kv_kernel(s_ref, g_ref, cos_ref, shi_ref, slo_ref, wk_ref, wv_ref,
               ckv_ref, kr_ref, *kv_out, with_kv):
    blk = s_ref[...]
    ckv = _rms(blk[:, :KV_LORA], g_ref[...])
    ckv_ref[...] = ckv
    tail = _rope(blk[:, KV_LORA:], cos_ref[...], shi_ref[...], slo_ref[...])
    kr_ref[...] = tail[:, S_ROPE - KV_LORA:S_ROPE - KV_LORA + QK_ROPE]
    if with_kv:
        kfull_ref, v_ref = kv_out
        ckv16 = ckv.astype(BF16)
        lhs = jnp.concatenate([ckv16, tail.astype(BF16)], axis=1)
        kfull_ref[...] = _dot(lhs, wk_ref[...]).astype(kfull_ref.dtype)
        v_ref[...] = _dot(ckv16, wv_ref[...]).astype(v_ref.dtype)


def kv_project(small, w, tabs, with_kv):
    n = small.shape[0]
    tm = min(_pick(n, (512, 256, 128)), tabs[0].shape[0])
    ntab = tabs[0].shape[0] // tm
    tab = pl.BlockSpec((tm, LANES), lambda i: (i % ntab, 0))
    out_specs = [pl.BlockSpec((tm, KV_LORA), lambda i: (i, 0)), pl.BlockSpec((tm, QK_ROPE), lambda i: (i, 0))]
    out_shape = [jax.ShapeDtypeStruct((n, KV_LORA), F32), jax.ShapeDtypeStruct((n, QK_ROPE), F32)]
    if with_kv:
        out_specs += [pl.BlockSpec((tm, MLA_HEADS * HEAD_PAD), lambda i: (i, 0)),
                      pl.BlockSpec((tm, MLA_HEADS * V_HEAD), lambda i: (i, 0))]
        out_shape += [jax.ShapeDtypeStruct((n, MLA_HEADS * HEAD_PAD), BF16),
                      jax.ShapeDtypeStruct((n, MLA_HEADS * V_HEAD), BF16)]
    return pl.pallas_call(
        functools.partial(_kv_kernel, with_kv=with_kv),
        grid=(n // tm,),
        in_specs=[
            pl.BlockSpec((tm, SMALL_DIM), lambda i: (i, 0)),
            pl.BlockSpec((1, KV_LORA), lambda i: (0, 0)),
            tab, tab, tab,
            pl.BlockSpec(w["wk_full"].shape, lambda i: (0, 0)),
            pl.BlockSpec(w["w_uv2"].shape, lambda i: (0, 0)),
        ],
        out_specs=out_specs,
        out_shape=out_shape,
        compiler_params=_cparams(("parallel",)),
        name="kv_project",
    )(small, w["kv_norm"], *tabs, w["wk_full"], w["w_uv2"])


def _q_kernel(x_ref, g_ref, cos_ref, shi_ref, slo_ref, wq_ref, *rest, absorb):
    u = _rms(x_ref[...].astype(F32), g_ref[...]).astype(BF16)
    q = _dot(u, wq_ref[...])
    tile = lambda r: jnp.concatenate([r[...]] * MLA_HEADS, axis=1)
    q = _rope(q, tile(cos_ref), tile(shi_ref), tile(slo_ref))
    if absorb:
        wabs_ref, q_ref, qa_ref = rest
        q_ref[...] = q.astype(q_ref.dtype)
        q16 = q.astype(BF16)
        for h in range(MLA_HEADS):
            qa_ref[:, h * KV_LORA:(h + 1) * KV_LORA] = _dot(
                q16[:, h * HEAD_PAD:(h + 1) * HEAD_PAD], wabs_ref[h]).astype(qa_ref.dtype)
    else:
        (q_ref,) = rest
        q_ref[...] = q.astype(q_ref.dtype)


def q_project(proj, w, tabs, absorb):
    n = proj.shape[0]
    tm = min(_pick(n, (512, 256, 128)), tabs[0].shape[0])
    ntab = tabs[0].shape[0] // tm
    tab = pl.BlockSpec((tm, LANES), lambda i: (i % ntab, 0))
    qw = MLA_HEADS * HEAD_PAD
    in_specs = [
        pl.BlockSpec((tm, Q_LORA), lambda i: (i, C_Q // Q_LORA)),
        pl.BlockSpec((1, Q_LORA), lambda i: (0, 0)),
        tab, tab, tab,
        pl.BlockSpec((Q_LORA, qw), lambda i: (0, 0)),
    ]
    args = [proj, w["q_norm"], *tabs, w["w_qb_pad"]]
    if absorb:
        in_specs.append(pl.BlockSpec((MLA_HEADS, HEAD_PAD, KV_LORA), lambda i: (0, 0, 0)))
        args.append(w["w_abs"])
        out_specs = [pl.BlockSpec((tm, qw), lambda i: (i, 0)),
                     pl.BlockSpec((tm, MLA_HEADS * KV_LORA), lambda i: (i, 0))]
        out_shape = [jax.ShapeDtypeStruct((n, qw), F32),
                     jax.ShapeDtypeStruct((n, MLA_HEADS * KV_LORA), F32)]
    else:
        out_specs = pl.BlockSpec((tm, qw), lambda i: (i, 0))
        out_shape = jax.ShapeDtypeStruct((n, qw), BF16)
    return pl.pallas_call(
        functools.partial(_q_kernel, absorb=absorb),
        grid=(n // tm,),
        in_specs=in_specs,
        out_specs=out_specs,
        out_shape=out_shape,
        compiler_params=_cparams(("parallel",)),
        name="q_project",
    )(*args)


def _bcast_lanes(x, width):
    return jnp.concatenate([x] * (width // LANES), axis=1)


def _prefill_attn_kernel(q_ref, k_ref, v_ref, o_ref, m_sc, l_sc, acc_sc):
    qi = pl.program_id(2)
    ki = pl.program_id(3)
    tq = q_ref.shape[0]
    tk = k_ref.shape[0]

    @pl.when(ki == 0)
    def _():
        m_sc[...] = jnp.full_like(m_sc, NEG_BIG)
        l_sc[...] = jnp.zeros_like(l_sc)
        acc_sc[...] = jnp.zeros_like(acc_sc)

    def update(masked):
        v = v_ref[...]
        low_half = lax.broadcasted_iota(jnp.int32, (tq, 2 * V_HEAD), 1) < V_HEAD
        alphas, pvs = [], []
        for h in range(2):
            qh = q_ref[:, h * HEAD_PAD:(h + 1) * HEAD_PAD]
            kh = k_ref[:, h * HEAD_PAD:(h + 1) * HEAD_PAD]
            s = _dot_nt(qh, kh) * ATTN_SCALE
            if masked:
                keep = (lax.broadcasted_iota(jnp.int32, (tq, tk), 0)
                        >= lax.broadcasted_iota(jnp.int32, (tq, tk), 1))
                s = jnp.where(keep, s, NEG_BIG)
            m_prev = m_sc[h]
            m_next = jnp.maximum(m_prev, jnp.max(s, axis=1, keepdims=True))
            p = jnp.exp(s - _bcast_lanes(m_next, tk))
            alpha = jnp.exp(m_prev - m_next)
            l_sc[h] = alpha * l_sc[h] + jnp.sum(p, axis=1, keepdims=True)
            m_sc[h] = m_next
            alphas.append(alpha)
            pvs.append(_dot(p.astype(BF16), v))
        acc_sc[...] = (jnp.where(low_half, alphas[0], alphas[1]) * acc_sc[...]
                       + jnp.where(low_half, pvs[0], pvs[1]))

    @pl.when(ki < qi)
    def _():
        update(False)

    @pl.when(ki == qi)
    def _():
        update(True)
        low_half = lax.broadcasted_iota(jnp.int32, (tq, 2 * V_HEAD), 1) < V_HEAD
        o_ref[...] = (acc_sc[...] / jnp.where(low_half, l_sc[0], l_sc[1])).astype(o_ref.dtype)


def prefill_attention(q, kfull, v, bs, seq):
    tq = _pick(seq, (512, 256, 128))
    nq = seq // tq
    n = bs * seq
    return pl.pallas_call(
        _prefill_attn_kernel,
        grid=(bs, MLA_HEADS // 2, nq, nq),
        in_specs=[
            pl.BlockSpec((tq, 2 * HEAD_PAD), lambda b, hp, qi, ki: (b * nq + qi, hp)),
            pl.BlockSpec((tq, 2 * HEAD_PAD), lambda b, hp, qi, ki: (b * nq + jnp.minimum(ki, qi), hp)),
            pl.BlockSpec((tq, 2 * V_HEAD), lambda b, hp, qi, ki: (b * nq + jnp.minimum(ki, qi), hp)),
        ],
        out_specs=pl.BlockSpec((tq, 2 * V_HEAD), lambda b, hp, qi, ki: (b * nq + qi, hp)),
        out_shape=jax.ShapeDtypeStruct((n, MLA_HEADS * V_HEAD), BF16),
        scratch_shapes=[pltpu.VMEM((2, tq, LANES), F32), pltpu.VMEM((2, tq, LANES), F32),
                        pltpu.VMEM((tq, 2 * V_HEAD), F32)],
        compiler_params=_cparams(("parallel", "parallel", "parallel", "arbitrary")),
        name="prefill_attention",
    )(q, kfull, v)


def _decode_attn_kernel(pt_ref, qa_ref, q_ref, cnew_ref, rnew_ref, *rest, pages):
    kv_refs = rest[:pages]
    kr_refs = rest[pages:2 * pages]
    o_ref, qa_sc, qr_sc, m_sc, l_sc, acc_sc = rest[2 * pages:]
    j = pl.program_id(1)
    seq = qa_ref.shape[0]
    rows = MLA_HEADS * seq
    page = kv_refs[0].shape[2]

    @pl.when(j == 0)
    def _():
        for h in range(MLA_HEADS):
            qa_sc[h * seq:(h + 1) * seq, :] = qa_ref[:, h * KV_LORA:(h + 1) * KV_LORA]
            qr_sc[h * seq:(h + 1) * seq, :] = q_ref[:, h * HEAD_PAD:(h + 1) * HEAD_PAD]
        m_sc[...] = jnp.full_like(m_sc, NEG_BIG)
        l_sc[...] = jnp.zeros_like(l_sc)
        acc_sc[...] = jnp.zeros_like(acc_sc)

    qa = qa_sc[...].astype(BF16)
    qr = qr_sc[:, QK_NOPE:QK_NOPE + QK_ROPE].astype(BF16)

    def update(scores, values):
        s = jnp.concatenate(scores, axis=1)
        m_prev = m_sc[...]
        m_next = jnp.maximum(m_prev, jnp.max(s, axis=1, keepdims=True))
        p = jnp.exp(s - _bcast_lanes(m_next, s.shape[1])).astype(BF16)
        alpha = jnp.exp(m_prev - m_next)
        l_sc[...] = alpha * l_sc[...] + jnp.sum(p.astype(F32), axis=1, keepdims=True)
        m_sc[...] = m_next
        pv = _dot(p[:, 0:page], values[0])
        for i in range(1, len(values)):
            pv = pv + _dot(p[:, i * page:(i + 1) * page], values[i])
        acc_sc[...] = _bcast_lanes(alpha, KV_LORA) * acc_sc[...] + pv

    scores, values = [], []
    for i in range(pages):
        c = kv_refs[i][0, 0].astype(BF16)
        r = kr_refs[i][0, 0].astype(BF16)
        scores.append((_dot_nt(qa, c) + _dot_nt(qr, r)) * ATTN_SCALE)
        values.append(c)
    update(scores, values)

    @pl.when(j == pl.num_programs(1) - 1)
    def _():
        pad = page - seq
        c = jnp.concatenate([cnew_ref[...], jnp.zeros((pad, KV_LORA), F32)], axis=0).astype(BF16)
        r = jnp.concatenate([rnew_ref[...], jnp.zeros((pad, QK_ROPE), F32)], axis=0).astype(BF16)
        s = (_dot_nt(qa, c) + _dot_nt(qr, r)) * ATTN_SCALE
        tok = lax.broadcasted_iota(jnp.int32, (rows, page), 0) % seq
        key = lax.broadcasted_iota(jnp.int32, (rows, page), 1)
        update([jnp.where(key <= tok, s, NEG_BIG)], [c])
        o = acc_sc[...] / _bcast_lanes(l_sc[...], KV_LORA)
        for h in range(MLA_HEADS):
            o_ref[:, h * KV_LORA:(h + 1) * KV_LORA] = o[h * seq:(h + 1) * seq, :].astype(o_ref.dtype)


def decode_attention(qa, q, ckv, kr, cache_kv, cache_kr, page_table, layer, bs, seq):
    n_pages = page_table.shape[1]
    page = cache_kv.shape[2]
    pages = _pick(n_pages, (16, 8, 4, 2))
    n = bs * seq
    rows = MLA_HEADS * seq

    def page_spec(width, i):
        return pl.BlockSpec((1, 1, page, width),
                            lambda b, j, pt: (layer, pt[b, j * pages + i], 0, 0))

    in_specs = [
        pl.BlockSpec((seq, MLA_HEADS * KV_LORA), lambda b, j, pt: (b, 0)),
        pl.BlockSpec((seq, MLA_HEADS * HEAD_PAD), lambda b, j, pt: (b, 0)),
        pl.BlockSpec((seq, KV_LORA), lambda b, j, pt: (b, 0)),
        pl.BlockSpec((seq, QK_ROPE), lambda b, j, pt: (b, 0)),
    ]
    in_specs += [page_spec(KV_LORA, i) for i in range(pages)]
    in_specs += [page_spec(QK_ROPE, i) for i in range(pages)]
    grid_spec = pltpu.PrefetchScalarGridSpec(
        num_scalar_prefetch=1,
        grid=(bs, n_pages // pages),
        in_specs=in_specs,
        out_specs=pl.BlockSpec((seq, MLA_HEADS * KV_LORA), lambda b, j, pt: (b, 0)),
        scratch_shapes=[pltpu.VMEM((rows, KV_LORA), F32), pltpu.VMEM((rows, HEAD_PAD), F32),
                        pltpu.VMEM((rows, LANES), F32), pltpu.VMEM((rows, LANES), F32),
                        pltpu.VMEM((rows, KV_LORA), F32)],
    )
    return pl.pallas_call(
        functools.partial(_decode_attn_kernel, pages=pages),
        grid_spec=grid_spec,
        out_shape=jax.ShapeDtypeStruct((n, MLA_HEADS * KV_LORA), F32),
        compiler_params=_cparams(("parallel", "arbitrary")),
        name="decode_attention",
    )(page_table, qa, q, ckv, kr, *([cache_kv] * pages), *([cache_kr] * pages))


def _merge_kernel(x_ref, ys_ref, ym_ref, gs_ref, gm_ref, ws_ref, wm_ref, wo_ref, o_ref):
    ms = _dot(ys_ref[...].astype(BF16), ws_ref[...])
    mm = _dot(ym_ref[...].astype(BF16), wm_ref[...])
    merged = jax.nn.sigmoid(gs_ref[...].astype(F32)) * ms + jax.nn.sigmoid(gm_ref[...].astype(F32)) * mm
    o_ref[...] = x_ref[...] + _dot(merged.astype(BF16), wo_ref[...])


def merge(x, y_ssm, y_mla, proj, w):
    n, d = x.shape
    tm = _pick(n, (256, 128))
    full = lambda a: pl.BlockSpec(a.shape, lambda i: (0, 0))
    return pl.pallas_call(
        _merge_kernel,
        grid=(n // tm,),
        in_specs=[
            pl.BlockSpec((tm, d), lambda i: (i, 0)),
            pl.BlockSpec((tm, D_INNER), lambda i: (i, 0)),
            pl.BlockSpec((tm, MLA_HEADS * V_HEAD), lambda i: (i, 0)),
            pl.BlockSpec((tm, d), lambda i: (i, C_GS // D_MODEL)),
            pl.BlockSpec((tm, d), lambda i: (i, C_GM // D_MODEL)),
            full(w["w_br_ssm"]), full(w["w_br_mla"]), full(w["w_out"]),
        ],
        out_specs=pl.BlockSpec((tm, d), lambda i: (i, 0)),
        out_shape=jax.ShapeDtypeStruct((n, d), F32),
        compiler_params=_cparams(("parallel",)),
        name="merge",
    )(x, y_ssm, y_mla, proj, proj, w["w_br_ssm"], w["w_br_mla"], w["w_out"])


def _ple_kernel(x_ref, p_ref, g_ref, wg_ref, wp_ref, fg_ref, o_ref, *, final):
    x = x_ref[...]
    gate = jax.nn.sigmoid(_dot(_rms(x, g_ref[...]).astype(BF16), wg_ref[...]))
    x = x + gate * _dot(p_ref[...].astype(BF16), wp_ref[...])
    if final:
        x = _rms(x, fg_ref[...])
    o_ref[...] = x


def ple(x, p, w, final_norm, final):
    n, d = x.shape
    tm = _pick(n, (512, 256, 128))
    full = lambda a: pl.BlockSpec(a.shape, lambda i: (0, 0))
    return pl.pallas_call(
        functools.partial(_ple_kernel, final=final),
        grid=(n // tm,),
        in_specs=[
            pl.BlockSpec((tm, d), lambda i: (i, 0)),
            pl.BlockSpec((tm, PLE_DIM), lambda i: (i, 0)),
            pl.BlockSpec((1, d), lambda i: (0, 0)),
            full(w["w_ple_gate"]), full(w["w_ple_proj"]),
            pl.BlockSpec((1, d), lambda i: (0, 0)),
        ],
        out_specs=pl.BlockSpec((tm, d), lambda i: (i, 0)),
        out_shape=jax.ShapeDtypeStruct((n, d), F32),
        compiler_params=_cparams(("parallel",)),
        name="ple",
    )(x, p, w["ple_norm"], w["w_ple_gate"], w["w_ple_proj"], final_norm)


def _rope_tables(pos, rows):
    half = QK_ROPE // 2
    inv = ROPE_BASE ** (-jnp.arange(half, dtype=F32) / half)
    ang = pos.astype(F32)[:, None] * inv[None, :]
    cos, sin = jnp.cos(ang), jnp.sin(ang)
    z = lambda w_: jnp.zeros((pos.shape[0], w_), F32)
    cos_t = jnp.concatenate([jnp.ones((pos.shape[0], QK_NOPE), F32), cos, cos, z(32)], axis=1)
    sin_hi = jnp.concatenate([z(QK_NOPE + half), sin, z(32)], axis=1)
    sin_lo = jnp.concatenate([z(QK_NOPE), -sin, z(half + 32)], axis=1)
    reps = max(1, rows // pos.shape[0])
    return tuple(jnp.tile(t, (reps, 1)) for t in (cos_t, sin_hi, sin_lo))


def _layer_weights(i, p):
    row = lambda v: v.reshape(1, -1).astype(F32)
    w_in = p["w_in"][i]
    off_z, off_xbc, off_dt, off_q, off_kv = 2048, 5120, 5152, 5664, 5952
    z = w_in[:, :off_z]
    xbc = w_in[:, off_z:off_xbc]
    dt = w_in[:, off_xbc:off_dt]
    q_lat = w_in[:, off_dt:off_q]
    kv_lat = w_in[:, off_q:off_q + KV_LORA]
    k_rope = w_in[:, off_q + KV_LORA:off_kv]
    gates = w_in[:, off_kv:]
    w_main = jnp.concatenate([z, xbc, gates, q_lat], axis=1).astype(BF16)
    w_main_x = jnp.concatenate([w_main, jnp.zeros((D_MODEL, C_DTX - MAIN_DIM), BF16),
                                jnp.repeat(dt, SSM_HEAD_DIM, axis=1).astype(BF16)], axis=1)
    zeros32 = jnp.zeros((D_MODEL, 32), F32)
    small = jnp.concatenate([kv_lat, dt, zeros32, k_rope, zeros32], axis=1)

    lane_pad = lambda v: jnp.pad(v.astype(F32), (0, LANES - v.shape[0])).reshape(1, LANES)
    a_neg = -jnp.exp(p["a_log"][i].astype(F32))

    w_qb = p["w_qb"][i].reshape(Q_LORA, MLA_HEADS, QK_NOPE + QK_ROPE)
    w_qb_pad = jnp.pad(w_qb, ((0, 0), (0, 0), (0, HEAD_PAD - QK_NOPE - QK_ROPE)))
    w_uk = p["w_uk"][i]
    wk_nope = jnp.pad(w_uk, ((0, 0), (0, 0), (0, HEAD_PAD - QK_NOPE))).reshape(KV_LORA, -1)
    place = jnp.zeros((LANES, MLA_HEADS, HEAD_PAD), F32)
    j = jnp.arange(QK_ROPE)
    place = place.at[S_ROPE - KV_LORA + j, :, QK_NOPE + j].set(1.0).reshape(LANES, -1)
    w_abs = jnp.pad(jnp.transpose(w_uk, (1, 2, 0)), ((0, 0), (0, HEAD_PAD - QK_NOPE), (0, 0)))
    w_uv = p["w_uv"][i]
    eye = jnp.eye(MLA_HEADS, dtype=F32)
    w_uv_bd = (jnp.transpose(w_uv, (1, 0, 2))[:, :, None, :] * eye[:, None, :, None]).reshape(
        MLA_HEADS * KV_LORA, MLA_HEADS * V_HEAD)

    return {
        "ffn1_norm": row(p["ffn1_norm"][i]), "ffn1_w_gu": p["ffn1_w_gu"][i].astype(BF16),
        "ffn1_w_down": p["ffn1_w_down"][i].astype(BF16),
        "ffn2_norm": row(p["ffn2_norm"][i]), "ffn2_w_gu": p["ffn2_w_gu"][i].astype(BF16),
        "ffn2_w_down": p["ffn2_w_down"][i].astype(BF16),
        "mix_norm": row(p["mix_norm"][i]),
        "w_main": w_main, "w_main_x": w_main_x,
        "w_small": small.astype(BF16),
        "conv_w": p["conv_w"][i].astype(F32), "conv_b": row(p["conv_b"][i]),
        "dt_bias_l": lane_pad(p["dt_bias"][i]), "a_neg_l": lane_pad(a_neg),
        "dt_bias_x": row(jnp.repeat(p["dt_bias"][i], SSM_HEAD_DIM)),
        "a_neg_x": row(jnp.repeat(a_neg, SSM_HEAD_DIM)),
        "d_skip_x": row(jnp.repeat(p["d_skip"][i], SSM_HEAD_DIM)),
        "ssm_norm": row(p["ssm_norm"][i]),
        "q_norm": row(p["q_norm"][i]), "kv_norm": row(p["kv_norm"][i]),
        "w_qb_pad": w_qb_pad.reshape(Q_LORA, -1).astype(BF16),
        "wk_full": jnp.concatenate([wk_nope, place], axis=0).astype(BF16),
        "w_uv2": w_uv.reshape(KV_LORA, -1).astype(BF16),
        "w_abs": w_abs.astype(BF16),
        "w_uv_bd": w_uv_bd.astype(BF16),
        "w_br_ssm": p["w_br_ssm"][i].astype(BF16), "w_br_mla": p["w_br_mla"][i].astype(BF16),
        "w_out": p["w_out"][i].astype(BF16),
        "ple_norm": row(p["ple_norm"][i]), "w_ple_gate": p["w_ple_gate"][i].astype(BF16),
        "w_ple_proj": p["w_ple_proj"][i].astype(BF16),
    }


def _trunk_layer(x, p_l, tabs, conv_prev, ssm_prev, w, bs, seq, final_norm, final, paged):
    x = ffn_half_step(x, w["ffn1_norm"], w["ffn1_w_gu"], w["ffn1_w_down"])
    proj = norm_matmul(x, w["mix_norm"], w["w_main"] if paged is None else w["w_main_x"], F32)
    small = norm_matmul(x, w["mix_norm"], w["w_small"], F32)
    if paged is None:
        y_ssm, new_conv, new_ssm = ssm_prefill(proj, small, conv_prev, ssm_prev, w, bs, seq)
        c_kv, k_rope, kfull, v = kv_project(small, w, tabs, True)
        q = q_project(proj, w, tabs, False)
        y_mla = prefill_attention(q, kfull, v, bs, seq)
    else:
        cache_kv, cache_kr, page_table, layer = paged
        y_ssm, new_conv, new_ssm = ssm_step(proj, conv_prev, ssm_prev, w, bs, seq)
        c_kv, k_rope = kv_project(small, w, tabs, False)
        q, qa = q_project(proj, w, tabs, True)
        o_lat = decode_attention(qa, q, c_kv, k_rope, cache_kv, cache_kr, page_table, layer, bs, seq)
        y_mla = norm_matmul(o_lat, None, w["w_uv_bd"], BF16, norm=False, tm_prefs=(256, 128))
    x = merge(x, y_ssm, y_mla, proj, w)
    x = ffn_half_step(x, w["ffn2_norm"], w["ffn2_w_gu"], w["ffn2_w_down"])
    x = ple(x, p_l, w, final_norm, final)
    return x, c_kv, k_rope, new_conv, new_ssm


def kernel(x_prompt, x_sample, cache_kv, cache_kr, state_conv, state_ssm, page_table, p_prompt, p_sample,
           ffn1_norm, ffn1_w_gu, ffn1_w_down, mix_norm, w_in, conv_w, conv_b, dt_bias, a_log, d_skip,
           ssm_norm, q_norm, w_qb, kv_norm, w_uk, w_uv, w_br_ssm, w_br_mla, w_out,
           ffn2_norm, ffn2_w_gu, ffn2_w_down, ple_norm, w_ple_gate, w_ple_proj, final_norm):
    params = dict(ffn1_norm=ffn1_norm, ffn1_w_gu=ffn1_w_gu, ffn1_w_down=ffn1_w_down, mix_norm=mix_norm,
                  w_in=w_in, conv_w=conv_w, conv_b=conv_b, dt_bias=dt_bias, a_log=a_log, d_skip=d_skip,
                  ssm_norm=ssm_norm, q_norm=q_norm, w_qb=w_qb, kv_norm=kv_norm, w_uk=w_uk, w_uv=w_uv,
                  w_br_ssm=w_br_ssm, w_br_mla=w_br_mla, w_out=w_out, ffn2_norm=ffn2_norm,
                  ffn2_w_gu=ffn2_w_gu, ffn2_w_down=ffn2_w_down, ple_norm=ple_norm,
                  w_ple_gate=w_ple_gate, w_ple_proj=w_ple_proj)
    depth = w_in.shape[0]
    bp, sp, d = x_prompt.shape
    bsm, ss, _ = x_sample.shape
    past_len = page_table.shape[1] * cache_kv.shape[2]
    np_, ns = bp * sp, bsm * ss

    tabs_p = _rope_tables(jnp.arange(sp), sp)
    tabs_s = _rope_tables(past_len + jnp.arange(ss), _pick(ns, (256, 128)))
    conv0 = jnp.zeros((bp, D_CONV - 1, CONV_DIM), F32)
    ssm0 = jnp.zeros((bp, SSM_HEADS, SSM_HEAD_DIM, SSM_STATE), F32)
    fnorm = final_norm.reshape(1, -1).astype(F32)

    hp = x_prompt.reshape(np_, d)
    hs = x_sample.reshape(ns, d)
    outs_p, outs_s = [], []
    for i in range(depth):
        last = i == depth - 1
        w = _layer_weights(i, params)
        hp, *rest_p = _trunk_layer(hp, p_prompt[i].reshape(np_, PLE_DIM), tabs_p, conv0, ssm0, w,
                                   bp, sp, fnorm, last, None)
        hs, *rest_s = _trunk_layer(hs, p_sample[i].reshape(ns, PLE_DIM), tabs_s, state_conv[i],
                                   state_ssm[i], w, bsm, ss, fnorm, last,
                                   (cache_kv, cache_kr, page_table, i))
        outs_p.append(rest_p)
        outs_s.append(rest_s)

    stack = lambda outs, k, shape: jnp.stack([o[k] for o in outs]).reshape((depth,) + shape)
    return (
        hp.reshape(bp, sp, d), hs.reshape(bsm, ss, d),
        stack(outs_p, 0, (bp, sp, KV_LORA)), stack(outs_p, 1, (bp, sp, QK_ROPE)),
        stack(outs_p, 2, (bp, D_CONV - 1, CONV_DIM)),
        stack(outs_p, 3, (bp, SSM_HEADS, SSM_HEAD_DIM, SSM_STATE)),
        stack(outs_s, 0, (bsm, ss, KV_LORA)), stack(outs_s, 1, (bsm, ss, QK_ROPE)),
        stack(outs_s, 2, (bsm, D_CONV - 1, CONV_DIM)),
        stack(outs_s, 3, (bsm, SSM_HEADS, SSM_HEAD_DIM, SSM_STATE)),
    )
```

```python
import functools
import math

import jax
import jax.numpy as jnp
from jax import lax
from jax.experimental import pallas as pl
from jax.experimental.pallas import tpu as pltpu

F32 = jnp.float32
BF16 = jnp.bfloat16

D_MODEL = 1024
D_FF = 2816
D_INNER = 2048
SSM_HEAD_DIM = 64
SSM_HEADS = 32
SSM_GROUPS = 4
HEADS_PER_GROUP = SSM_HEADS // SSM_GROUPS
SSM_STATE = 128
GROUP_CH = D_INNER // SSM_GROUPS
D_CONV = 4
CONV_DIM = D_INNER + 2 * SSM_GROUPS * SSM_STATE
SSD_CHUNK = 128
MLA_HEADS = 16
Q_LORA = 512
KV_LORA = 256
QK_NOPE = 64
QK_ROPE = 32
V_HEAD = 64
ROPE_BASE = 10000.0
ATTN_SCALE = (QK_NOPE + QK_ROPE) ** -0.5
SCORE_SCALE = ATTN_SCALE * math.log2(math.e)
PLE_DIM = 256
EPS = 1e-6
LANES = 128
HEAD_PAD = 128
DECODE_CHUNK_PAGES = 8
NEG_BIG = -1e30

C_Z, C_XS, C_BC, C_GS, C_GM, C_Q = 0, 2048, 4096, 5120, 6144, 7168
MAIN_DIM = 7680
C_DTX = 8192
MAIN_DIM_X = C_DTX + D_INNER
SMALL_DIM = 384
S_DT, S_ROPE = 256, 320

VMEM_LIMIT = 56 * 1024 * 1024


def _cparams(sem):
    return pltpu.CompilerParams(dimension_semantics=sem, vmem_limit_bytes=VMEM_LIMIT)


def _pick(n, prefs):
    for p in prefs:
        if n % p == 0:
            return p
    return n


def _dot(a, b):
    return jnp.dot(a, b, preferred_element_type=F32)


def _dot_nt(a, b):
    return lax.dot_general(a, b, (((1,), (1,)), ((), ())), preferred_element_type=F32)


def _silu(x):
    return x * jax.nn.sigmoid(x)


def _softplus(x):
    return jnp.maximum(x, 0.0) + jnp.log1p(jnp.exp(-jnp.abs(x)))


def _rms(x, g):
    return x * lax.rsqrt(jnp.mean(x * x, axis=-1, keepdims=True) + EPS) * g


def _nm_kernel(x_ref, g_ref, w_ref, o_ref, u_sc, *, norm):
    @pl.when(pl.program_id(1) == 0)
    def _():
        x = x_ref[...].astype(F32)
        if norm:
            x = _rms(x, g_ref[...])
        u_sc[...] = x.astype(BF16)

    o_ref[...] = _dot(u_sc[...], w_ref[...]).astype(o_ref.dtype)


def norm_matmul(x, g, w, out_dtype, *, norm=True, col_block=0, tm_prefs=(1024, 512, 256, 128)):
    n = x.shape[0]
    k, nout = w.shape
    tm = _pick(n, tm_prefs)
    tn = _pick(nout, (1280, 1024, 512, 384, 256, 128))
    if not norm:
        g = jnp.ones((1, k), F32)
    return pl.pallas_call(
        functools.partial(_nm_kernel, norm=norm),
        grid=(n // tm, nout // tn),
        in_specs=[
            pl.BlockSpec((tm, k), lambda i, j: (i, col_block)),
            pl.BlockSpec((1, k), lambda i, j: (0, 0)),
            pl.BlockSpec((k, tn), lambda i, j: (0, j)),
        ],
        out_specs=pl.BlockSpec((tm, tn), lambda i, j: (i, j)),
        out_shape=jax.ShapeDtypeStruct((n, nout), out_dtype),
        scratch_shapes=[pltpu.VMEM((tm, k), BF16)],
        compiler_params=_cparams(("parallel", "arbitrary")),
        name="norm_matmul",
    )(x, g, w)


def _ffn_kernel(x_ref, g_ref, wgu_ref, wd_ref, o_ref):
    x = x_ref[...]
    dff = wd_ref.shape[0]
    u = _rms(x, g_ref[...]).astype(BF16)
    a = _dot(u, wgu_ref[:, :dff])
    b = _dot(u, wgu_ref[:, dff:])
    h = (_silu(a) * b).astype(BF16)
    o_ref[...] = x + 0.5 * _dot(h, wd_ref[...])


def _resident(shape):
    return pl.BlockSpec(shape, lambda *_: (0,) * len(shape), pipeline_mode=pl.Buffered(1))


def ffn_half_step(x, g, w_gu, w_down):
    n, d = x.shape
    tm = _pick(n, (512, 256, 128))
    return pl.pallas_call(
        _ffn_kernel,
        grid=(n // tm,),
        in_specs=[
            pl.BlockSpec((tm, d), lambda i: (i, 0)),
            pl.BlockSpec((1, d), lambda i: (0, 0)),
            _resident(w_gu.shape),
            _resident(w_down.shape),
        ],
        out_specs=pl.BlockSpec((tm, d), lambda i: (i, 0)),
        out_shape=jax.ShapeDtypeStruct((n, d), F32),
        compiler_params=_cparams(("parallel",)),
        name="ffn_half_step",
    )(x, g, w_gu, w_down)


def _causal_conv(xpad_sc, rows, w_ref, b_ref):
    w = w_ref[...]
    acc = xpad_sc[5:5 + rows, :] * w[0:1, :]
    for k in range(1, D_CONV):
        acc = acc + xpad_sc[5 + k:5 + k + rows, :] * w[k:k + 1, :]
    return _silu(acc + b_ref[...])


def _gate_and_group_norm(y, z, norm_w):
    y = y * _silu(z)
    parts = []
    for g in range(SSM_GROUPS):
        yg = y[:, g * GROUP_CH:(g + 1) * GROUP_CH]
        parts.append(yg * lax.rsqrt(jnp.mean(yg * yg, axis=-1, keepdims=True) + EPS))
    return jnp.concatenate(parts, axis=1) * norm_w


def _cumsum_rows(a):
    rows = a.shape[0]
    row = lax.broadcasted_iota(jnp.int32, a.shape, 0)
    s = 1
    while s < rows:
        a = a + jnp.where(row >= s, pltpu.roll(a, s, axis=0), 0.0)
        s *= 2
    return a


def _ssm_prefill_kernel(z_ref, xs_ref, bc_ref, dt_ref, cprev_ref, sprev_ref, convw_ref, convb_ref,
                        dtb_ref, aneg_ref, dskip_ref, norm_ref,
                        y_ref, nconv_ref, nssm_ref, xpad_sc, h_sc):
    c = pl.program_id(1)
    t = SSD_CHUNK
    hp = SSM_HEAD_DIM

    @pl.when(c == 0)
    def _():
        xpad_sc[5:8, :] = cprev_ref[0]
        h_sc[...] = sprev_ref[0].reshape(D_INNER, SSM_STATE)

    xpad_sc[8:8 + t, 0:D_INNER] = xs_ref[...]
    xpad_sc[8:8 + t, D_INNER:CONV_DIM] = bc_ref[...]
    conv = _causal_conv(xpad_sc, t, convw_ref, convb_ref)
    tail = xpad_sc[8 + t - 3:8 + t, :]
    xpad_sc[5:8, :] = tail

    xs = conv[:, :D_INNER]
    gs = SSM_GROUPS * SSM_STATE
    bm = conv[:, D_INNER:D_INNER + gs]
    cm = conv[:, D_INNER + gs:]

    dt = _softplus(dt_ref[...] + dtb_ref[...])
    acum = _cumsum_rows(dt * aneg_ref[...])
    alast = acum[t - 1:t, :]
    acum_t = acum.T
    dt_t = dt.T
    wst_t = (dt * jnp.exp(alast - acum)).T
    xs_t = xs.T

    tri = (lax.broadcasted_iota(jnp.int32, (t, t), 0) >= lax.broadcasted_iota(jnp.int32, (t, t), 1))
    low_half = lax.broadcasted_iota(jnp.int32, (t, 2 * hp), 1) < hp

    y_parts = []
    for g in range(SSM_GROUPS):
        bg = bm[:, g * SSM_STATE:(g + 1) * SSM_STATE]
        cg = cm[:, g * SSM_STATE:(g + 1) * SSM_STATE]
        cb = _dot_nt(cg.astype(BF16), bg.astype(BF16))
        for pair in range(HEADS_PER_GROUP // 2):
            h0 = g * HEADS_PER_GROUP + 2 * pair
            lhs = []
            for h in (h0, h0 + 1):
                colb = jnp.broadcast_to(acum[:, h:h + 1], (t, t))
                decay = jnp.where(tri, jnp.exp(colb - acum_t[h:h + 1, :]), 0.0)
                m = cb * decay * dt_t[h:h + 1, :]
                ce = cg * jnp.exp(colb)
                lhs.append(jnp.concatenate([m, ce], axis=1))
            lhs = jnp.concatenate(lhs, axis=0).astype(BF16)
            r0 = h0 * hp
            rhs_t = jnp.concatenate([xs_t[r0:r0 + 2 * hp, :], h_sc[r0:r0 + 2 * hp, :]],
                                    axis=1).astype(BF16)
            out = _dot_nt(lhs, rhs_t)
            y_parts.append(jnp.where(low_half, out[:t, :], out[t:, :]))
    y = jnp.concatenate(y_parts, axis=1) + dskip_ref[...] * xs
    y_ref[...] = _gate_and_group_norm(y, z_ref[...], norm_ref[...]).astype(y_ref.dtype)

    elast = jnp.exp(alast)
    for g in range(SSM_GROUPS):
        bg = bm[:, g * SSM_STATE:(g + 1) * SSM_STATE].astype(BF16)
        xw = []
        for hh in range(HEADS_PER_GROUP):
            h = g * HEADS_PER_GROUP + hh
            xw.append(xs_t[h * hp:(h + 1) * hp, :] * wst_t[h:h + 1, :])
        dh = _dot(jnp.concatenate(xw, axis=0).astype(BF16), bg)
        for hh in range(HEADS_PER_GROUP):
            h = g * HEADS_PER_GROUP + hh
            rows = slice(h * hp, (h + 1) * hp)
            h_sc[rows, :] = h_sc[rows, :] * elast[:, h:h + 1] + dh[hh * hp:(hh + 1) * hp, :]

    @pl.when(c == pl.num_programs(1) - 1)
    def _():
        nconv_ref[0] = tail
        nssm_ref[0] = h_sc[...].reshape(SSM_HEADS, SSM_HEAD_DIM, SSM_STATE)


def ssm_prefill(proj, small, conv_prev, ssm_prev, w, bs, seq):
    t = SSD_CHUNK
    nc = seq // t
    n = bs * seq
    row = lambda b, c: b * nc + c
    vec = lambda width: pl.BlockSpec((1, width), lambda b, c: (0, 0))
    return pl.pallas_call(
        _ssm_prefill_kernel,
        grid=(bs, nc),
        in_specs=[
            pl.BlockSpec((t, D_INNER), lambda b, c: (row(b, c), C_Z // D_INNER)),
            pl.BlockSpec((t, D_INNER), lambda b, c: (row(b, c), C_XS // D_INNER)),
            pl.BlockSpec((t, 1024), lambda b, c: (row(b, c), C_BC // 1024)),
            pl.BlockSpec((t, LANES), lambda b, c: (row(b, c), S_DT // LANES)),
            pl.BlockSpec((1, D_CONV - 1, CONV_DIM), lambda b, c: (b, 0, 0)),
            pl.BlockSpec((1, SSM_HEADS, SSM_HEAD_DIM, SSM_STATE), lambda b, c: (b, 0, 0, 0)),
            pl.BlockSpec((D_CONV, CONV_DIM), lambda b, c: (0, 0)),
            vec(CONV_DIM), vec(LANES), vec(LANES), vec(D_INNER), vec(D_INNER),
        ],
        out_specs=[
            pl.BlockSpec((t, D_INNER), lambda b, c: (row(b, c), 0)),
            pl.BlockSpec((1, D_CONV - 1, CONV_DIM), lambda b, c: (b, 0, 0)),
            pl.BlockSpec((1, SSM_HEADS, SSM_HEAD_DIM, SSM_STATE), lambda b, c: (b, 0, 0, 0)),
        ],
        out_shape=[
            jax.ShapeDtypeStruct((n, D_INNER), F32),
            jax.ShapeDtypeStruct((bs, D_CONV - 1, CONV_DIM), F32),
            jax.ShapeDtypeStruct((bs, SSM_HEADS, SSM_HEAD_DIM, SSM_STATE), F32),
        ],
        scratch_shapes=[pltpu.VMEM((t + 8, CONV_DIM), F32), pltpu.VMEM((D_INNER, SSM_STATE), F32)],
        compiler_params=_cparams(("parallel", "arbitrary")),
        name="ssm_prefill",
    )(proj, proj, proj, small, conv_prev, ssm_prev, w["conv_w"], w["conv_b"], w["dt_bias_l"],
      w["a_neg_l"], w["d_skip_x"], w["ssm_norm"])


def _ssm_step_kernel(z_ref, xs_ref, bc_ref, dtx_ref, cprev_ref, sprev_ref, convw_ref, convb_ref,
                     dtbx_ref, anegx_ref, dskip_ref, norm_ref, *rest):
    y_ref, nconv_ref, nssm_ref, xpad_sc, xw_sc, bpad_sc = rest[-6:]
    seq = xs_ref.shape[0]
    hp = SSM_HEAD_DIM
    gs = SSM_GROUPS * SSM_STATE

    @pl.when(pl.program_id(0) == 0)
    def _():
        xw_sc[...] = jnp.zeros_like(xw_sc)
        bpad_sc[...] = jnp.zeros_like(bpad_sc)

    xpad_sc[5:8, :] = cprev_ref[0, 0]
    xpad_sc[8:8 + seq, 0:D_INNER] = xs_ref[...]
    xpad_sc[8:8 + seq, D_INNER:CONV_DIM] = bc_ref[...]
    conv = _causal_conv(xpad_sc, seq, convw_ref, convb_ref)
    nconv_ref[0] = xpad_sc[8 + seq - 3:8 + seq, :]

    xs = conv[:, :D_INNER]
    bm = conv[:, D_INNER:D_INNER + gs]
    cm = conv[:, D_INNER + gs:]

    dtx = _softplus(dtx_ref[...] + dtbx_ref[...])
    acum = _cumsum_rows(dtx * anegx_ref[...])
    alast = acum[seq - 1:seq, :]
    xd = xs * dtx

    c_all = jnp.concatenate([cm[:, g * SSM_STATE:(g + 1) * SSM_STATE] for g in range(SSM_GROUPS)],
                            axis=0).astype(BF16)
    y_off = []
    for g in range(SSM_GROUPS):
        hg = sprev_ref[0, 0, g * HEADS_PER_GROUP:(g + 1) * HEADS_PER_GROUP].reshape(GROUP_CH, SSM_STATE)
        y_off.append(_dot_nt(c_all, hg.astype(BF16))[g * seq:(g + 1) * seq, :])
    y = jnp.concatenate(y_off, axis=1) * jnp.exp(acum)

    ones = jnp.ones((SSM_STATE, GROUP_CH), BF16)
    cbx = []
    for g in range(SSM_GROUPS):
        bg = bm[:, g * SSM_STATE:(g + 1) * SSM_STATE]
        cg = cm[:, g * SSM_STATE:(g + 1) * SSM_STATE]
        prod = jnp.concatenate([cg * bg[s:s + 1, :] for s in range(seq)], axis=0)
        hi = prod.astype(BF16)
        lo = (prod - hi.astype(F32)).astype(BF16)
        cbx.append(_dot(hi, ones) + _dot(lo, ones))
    row = lax.broadcasted_iota(jnp.int32, (seq, D_INNER), 0)
    for s in range(seq):
        cb_s = jnp.concatenate([cbx[g][s * seq:(s + 1) * seq, :] for g in range(SSM_GROUPS)], axis=1)
        decay = jnp.where(row >= s, jnp.exp(acum - acum[s:s + 1, :]), 0.0)
        y = y + cb_s * decay * xd[s:s + 1, :]

    y = y + dskip_ref[...] * xs
    y_ref[...] = _gate_and_group_norm(y, z_ref[...], norm_ref[...]).astype(y_ref.dtype)

    xw_sc[0:seq, :] = xd * jnp.exp(alast - acum)
    bpad_sc[0:seq, :] = bm
    xw_t = xw_sc[...].T
    elast = jnp.exp(alast)
    for g in range(SSM_GROUPS):
        dh = _dot(xw_t[g * GROUP_CH:(g + 1) * GROUP_CH, :].astype(BF16),
                  bpad_sc[:, g * SSM_STATE:(g + 1) * SSM_STATE].astype(BF16))
        for hh in range(HEADS_PER_GROUP):
            h = g * HEADS_PER_GROUP + hh
            nssm_ref[0, 0, h] = (sprev_ref[0, 0, h] * elast[:, h * hp:h * hp + 1]
                                 + dh[hh * hp:(hh + 1) * hp, :])


def ssm_step(proj, state_conv, state_ssm, new_ssm_stack, layer, w, bs, seq):
    n = bs * seq
    vec = lambda width: pl.BlockSpec((1, width), lambda b: (0, 0))
    state_block = (1, 1, SSM_HEADS, SSM_HEAD_DIM, SSM_STATE)
    in_specs = [
        pl.BlockSpec((seq, D_INNER), lambda b: (b, C_Z // D_INNER)),
        pl.BlockSpec((seq, D_INNER), lambda b: (b, C_XS // D_INNER)),
        pl.BlockSpec((seq, 1024), lambda b: (b, C_BC // 1024)),
        pl.BlockSpec((seq, D_INNER), lambda b: (b, C_DTX // D_INNER)),
        pl.BlockSpec((1, 1, D_CONV - 1, CONV_DIM), lambda b: (layer, b, 0, 0)),
        pl.BlockSpec(state_block, lambda b: (layer, b, 0, 0, 0)),
        pl.BlockSpec((D_CONV, CONV_DIM), lambda b: (0, 0)),
        vec(CONV_DIM), vec(D_INNER), vec(D_INNER), vec(D_INNER), vec(D_INNER),
    ]
    args = [proj, proj, proj, proj, state_conv, state_ssm, w["conv_w"], w["conv_b"], w["dt_bias_x"],
            w["a_neg_x"], w["d_skip_x"], w["ssm_norm"]]
    aliases = {}
    if new_ssm_stack is not None:
        in_specs.append(pl.BlockSpec(memory_space=pl.ANY))
        args.append(new_ssm_stack)
        aliases = {len(args) - 1: 2}
    return pl.pallas_call(
        _ssm_step_kernel,
        grid=(bs,),
        in_specs=in_specs,
        out_specs=[
            pl.BlockSpec((seq, D_INNER), lambda b: (b, 0)),
            pl.BlockSpec((1, D_CONV - 1, CONV_DIM), lambda b: (b, 0, 0)),
            pl.BlockSpec(state_block, lambda b: (layer, b, 0, 0, 0)),
        ],
        out_shape=[
            jax.ShapeDtypeStruct((n, D_INNER), F32),
            jax.ShapeDtypeStruct((bs, D_CONV - 1, CONV_DIM), F32),
            jax.ShapeDtypeStruct(state_ssm.shape, F32),
        ],
        scratch_shapes=[pltpu.VMEM((seq + 8, CONV_DIM), F32), pltpu.VMEM((LANES, D_INNER), F32),
                        pltpu.VMEM((LANES, SSM_GROUPS * SSM_STATE), F32)],
        input_output_aliases=aliases,
        compiler_params=_cparams(("arbitrary",)),
        name="ssm_step",
    )(*args)


def _rope(x, cos, sin_hi, sin_lo):
    return x * cos + pltpu.roll(x, 16, axis=1) * sin_hi + pltpu.roll(x, x.shape[1] - 16, axis=1) * sin_lo


def _kv_kernel(s_ref, g_ref, cos_ref, shi_ref, slo_ref, wk_ref, wv_ref,
               ckv_ref, kr_ref, *kv_out, with_kv):
    blk = s_ref[...]
    ckv = _rms(blk[:, :KV_LORA], g_ref[...])
    ckv_ref[...] = ckv
    tail = _rope(blk[:, KV_LORA:], cos_ref[...], shi_ref[...], slo_ref[...])
    kr_ref[...] = tail[:, S_ROPE - KV_LORA:S_ROPE - KV_LORA + QK_ROPE]
    if with_kv:
        kfull_ref, vt_ref = kv_out
        ckv16 = ckv.astype(BF16)
        lhs = jnp.concatenate([ckv16, tail.astype(BF16)], axis=1)
        kfull_ref[...] = _dot(lhs, wk_ref[...]).astype(kfull_ref.dtype)
        vt_ref[0] = _dot_nt(wv_ref[...], ckv16).astype(vt_ref.dtype)


def kv_project(small, w, tabs, with_kv):
    n = small.shape[0]
    tm = min(_pick(n, (512, 256, 128)), tabs[0].shape[0])
    ntab = tabs[0].shape[0] // tm
    tab = pl.BlockSpec((tm, LANES), lambda i: (i % ntab, 0))
    out_specs = [pl.BlockSpec((tm, KV_LORA), lambda i: (i, 0)), pl.BlockSpec((tm, QK_ROPE), lambda i: (i, 0))]
    out_shape = [jax.ShapeDtypeStruct((n, KV_LORA), F32), jax.ShapeDtypeStruct((n, QK_ROPE), F32)]
    if with_kv:
        out_specs += [pl.BlockSpec((tm, MLA_HEADS * HEAD_PAD), lambda i: (i, 0)),
                      pl.BlockSpec((1, MLA_HEADS * V_HEAD, tm), lambda i: (i, 0, 0))]
        out_shape += [jax.ShapeDtypeStruct((n, MLA_HEADS * HEAD_PAD), BF16),
                      jax.ShapeDtypeStruct((n // tm, MLA_HEADS * V_HEAD, tm), BF16)]
    return pl.pallas_call(
        functools.partial(_kv_kernel, with_kv=with_kv),
        grid=(n // tm,),
        in_specs=[
            pl.BlockSpec((tm, SMALL_DIM), lambda i: (i, 0)),
            pl.BlockSpec((1, KV_LORA), lambda i: (0, 0)),
            tab, tab, tab,
            pl.BlockSpec(w["wk_full"].shape, lambda i: (0, 0)),
            pl.BlockSpec(w["w_uv_t"].shape, lambda i: (0, 0)),
        ],
        out_specs=out_specs,
        out_shape=out_shape,
        compiler_params=_cparams(("parallel",)),
        name="kv_project",
    )(small, w["kv_norm"], *tabs, w["wk_full"], w["w_uv_t"])


def _q_kernel(x_ref, g_ref, cos_ref, shi_ref, slo_ref, wq_ref, *rest, absorb):
    u = _rms(x_ref[...].astype(F32), g_ref[...]).astype(BF16)
    q = _dot(u, wq_ref[...])
    tile = lambda r: jnp.concatenate([r[...]] * MLA_HEADS, axis=1)
    q = _rope(q, tile(cos_ref), tile(shi_ref), tile(slo_ref))
    if absorb:
        wabs_ref, q_ref, qa_ref = rest
        q_ref[...] = q.astype(q_ref.dtype)
        q16 = q.astype(BF16)
        for h in range(MLA_HEADS):
            qa_ref[:, h * KV_LORA:(h + 1) * KV_LORA] = _dot(
                q16[:, h * HEAD_PAD:(h + 1) * HEAD_PAD], wabs_ref[h]).astype(qa_ref.dtype)
    else:
        (q_ref,) = rest
        q_ref[...] = (q * SCORE_SCALE).astype(q_ref.dtype)


def q_project(proj, w, tabs, absorb):
    n = proj.shape[0]
    tm = min(_pick(n, (512, 256, 128)), tabs[0].shape[0])
    ntab = tabs[0].shape[0] // tm
    tab = pl.BlockSpec((tm, LANES), lambda i: (i % ntab, 0))
    qw = MLA_HEADS * HEAD_PAD
    in_specs = [
        pl.BlockSpec((tm, Q_LORA), lambda i: (i, C_Q // Q_LORA)),
        pl.BlockSpec((1, Q_LORA), lambda i: (0, 0)),
        tab, tab, tab,
        pl.BlockSpec((Q_LORA, qw), lambda i: (0, 0)),
    ]
    args = [proj, w["q_norm"], *tabs, w["w_qb_pad"]]
    if absorb:
        in_specs.append(pl.BlockSpec((MLA_HEADS, HEAD_PAD, KV_LORA), lambda i: (0, 0, 0)))
        args.append(w["w_abs"])
        out_specs = [pl.BlockSpec((tm, qw), lambda i: (i, 0)),
                     pl.BlockSpec((tm, MLA_HEADS * KV_LORA), lambda i: (i, 0))]
        out_shape = [jax.ShapeDtypeStruct((n, qw), F32),
                     jax.ShapeDtypeStruct((n, MLA_HEADS * KV_LORA), F32)]
    else:
        out_specs = pl.BlockSpec((tm, qw), lambda i: (i, 0))
        out_shape = jax.ShapeDtypeStruct((n, qw), BF16)
    return pl.pallas_call(
        functools.partial(_q_kernel, absorb=absorb),
        grid=(n // tm,),
        in_specs=in_specs,
        out_specs=out_specs,
        out_shape=out_shape,
        compiler_params=_cparams(("parallel",)),
        name="q_project",
    )(*args)


def _bcast_lanes(x, width):
    return jnp.concatenate([x] * (width // LANES), axis=1)


def _prefill_attn_kernel(q_ref, k_ref, vt_ref, o_ref, m_sc, l_sc, acc_sc):
    qi = pl.program_id(2)
    tq = q_ref.shape[0]
    tk = vt_ref.shape[2]
    nsplit, _, tc = acc_sc.shape
    head0 = lax.broadcasted_iota(jnp.int32, (2 * V_HEAD, tc), 0) < V_HEAD

    m_sc[...] = jnp.full_like(m_sc, NEG_BIG)
    l_sc[...] = jnp.zeros_like(l_sc)
    acc_sc[...] = jnp.zeros_like(acc_sc)

    def update(ki, masked):
        vt = vt_ref[ki]
        start = pl.multiple_of(ki * tk, tk)
        scores = []
        for c in range(nsplit):
            for h in range(2):
                qh = q_ref[c * tc:(c + 1) * tc, h * HEAD_PAD:(h + 1) * HEAD_PAD]
                kh = k_ref[pl.ds(start, tk), h * HEAD_PAD:(h + 1) * HEAD_PAD]
                scores.append(_dot_nt(kh, qh))
        for c in range(nsplit):
            alphas, pvs = [], []
            for h in range(2):
                st = c * 2 + h
                s = scores[st]
                if masked:
                    keep = (lax.broadcasted_iota(jnp.int32, (tk, tc), 0)
                            <= lax.broadcasted_iota(jnp.int32, (tk, tc), 1) + c * tc)
                    s = jnp.where(keep, s, NEG_BIG)
                m_prev = m_sc[st]
                m_next = jnp.maximum(m_prev, jnp.max(s, axis=0, keepdims=True))
                p = jnp.exp2(s - m_next)
                alpha = jnp.exp2(m_prev - m_next)
                l_sc[st] = alpha * l_sc[st] + jnp.sum(p, axis=0, keepdims=True)
                m_sc[st] = m_next
                alphas.append(alpha)
                pvs.append(_dot(vt, p.astype(BF16)))
            acc_sc[c] = (jnp.where(head0, alphas[0], alphas[1]) * acc_sc[c]
                         + jnp.where(head0, pvs[0], pvs[1]))

    def body(ki, carry):
        update(ki, False)
        return carry

    lax.fori_loop(0, qi, body, 0)
    update(qi, True)
    for c in range(nsplit):
        o_t = acc_sc[c] / jnp.where(head0, l_sc[2 * c], l_sc[2 * c + 1])
        o_ref[c * tc:(c + 1) * tc, :] = o_t.T.astype(o_ref.dtype)


def prefill_attention(q, kfull, vt, bs, seq):
    tk = vt.shape[2]
    tq = tk
    nq = seq // tq
    n = bs * seq
    nsplit = 2 if tq % (2 * LANES) == 0 else 1
    tc = tq // nsplit
    return pl.pallas_call(
        _prefill_attn_kernel,
        grid=(bs, MLA_HEADS // 2, nq),
        in_specs=[
            pl.BlockSpec((tq, 2 * HEAD_PAD), lambda b, hp, qi: (b * nq + qi, hp)),
            pl.BlockSpec((seq, 2 * HEAD_PAD), lambda b, hp, qi: (b, hp)),
            pl.BlockSpec((nq, 2 * V_HEAD, tk), lambda b, hp, qi: (b, hp, 0)),
        ],
        out_specs=pl.BlockSpec((tq, 2 * V_HEAD), lambda b, hp, qi: (b * nq + qi, hp)),
        out_shape=jax.ShapeDtypeStruct((n, MLA_HEADS * V_HEAD), BF16),
        scratch_shapes=[pltpu.VMEM((2 * nsplit, 1, tc), F32), pltpu.VMEM((2 * nsplit, 1, tc), F32),
                        pltpu.VMEM((nsplit, 2 * V_HEAD, tc), F32)],
        compiler_params=_cparams(("parallel", "parallel", "arbitrary")),
        name="prefill_attention",
    )(q, kfull, vt)


def _decode_attn_kernel(pt_ref, qa_ref, q_ref, cnew_ref, rnew_ref, *rest, pages):
    kv_refs = rest[:pages]
    kr_refs = rest[pages:2 * pages]
    o_ref, qa_sc, qr_sc, c_sc, r_sc, m_sc, l_sc, acc_sc = rest[2 * pages:]
    j = pl.program_id(1)
    seq = qa_ref.shape[0]
    rows = MLA_HEADS * seq
    page = kv_refs[0].shape[2]

    @pl.when(j == 0)
    def _():
        for h in range(MLA_HEADS):
            qa_sc[h * seq:(h + 1) * seq, :] = qa_ref[:, h * KV_LORA:(h + 1) * KV_LORA] * SCORE_SCALE
            qr_sc[h * seq:(h + 1) * seq, :] = q_ref[:, h * HEAD_PAD:(h + 1) * HEAD_PAD] * SCORE_SCALE
        m_sc[...] = jnp.full_like(m_sc, NEG_BIG)
        l_sc[...] = jnp.zeros_like(l_sc)
        acc_sc[...] = jnp.zeros_like(acc_sc)

    qa = qa_sc[...].astype(BF16)
    qr = qr_sc[:, QK_NOPE:QK_NOPE + QK_ROPE].astype(BF16)

    def update(s, values):
        m_prev = m_sc[...]
        m_next = jnp.maximum(m_prev, jnp.max(s, axis=1, keepdims=True))
        p = jnp.exp2(s - _bcast_lanes(m_next, s.shape[1]))
        alpha = jnp.exp2(m_prev - m_next)
        l_sc[...] = alpha * l_sc[...] + jnp.sum(p, axis=1, keepdims=True)
        m_sc[...] = m_next
        acc_sc[...] = _bcast_lanes(alpha, KV_LORA) * acc_sc[...] + _dot(p.astype(BF16), values)

    for i in range(pages):
        c_sc[i * page:(i + 1) * page, :] = kv_refs[i][0, 0].astype(BF16)
        r_sc[:, i * page:(i + 1) * page] = kr_refs[i][0, 0].astype(BF16)
    chunk = min(pages, DECODE_CHUNK_PAGES) * page
    spans = [slice(st, st + chunk) for st in range(0, pages * page, chunk)]
    scores = [_dot_nt(qa, c_sc[sp, :]) + _dot(qr, r_sc[:, sp]) for sp in spans]
    for sp, s in zip(spans, scores):
        update(s, c_sc[sp, :])

    @pl.when(j == pl.num_programs(1) - 1)
    def _():
        pad = page - seq
        c = jnp.concatenate([cnew_ref[...], jnp.zeros((pad, KV_LORA), F32)], axis=0).astype(BF16)
        r = jnp.concatenate([rnew_ref[...], jnp.zeros((pad, QK_ROPE), F32)], axis=0).astype(BF16)
        s = _dot_nt(qa, c) + _dot_nt(qr, r)
        tok = lax.broadcasted_iota(jnp.int32, (rows, page), 0) % seq
        key = lax.broadcasted_iota(jnp.int32, (rows, page), 1)
        update(jnp.where(key <= tok, s, NEG_BIG), c)
        o = acc_sc[...] / _bcast_lanes(l_sc[...], KV_LORA)
        for h in range(MLA_HEADS):
            o_ref[:, h * KV_LORA:(h + 1) * KV_LORA] = o[h * seq:(h + 1) * seq, :].astype(o_ref.dtype)


def decode_attention(qa, q, ckv, kr, cache_kv, cache_kr_t, page_table, layer, bs, seq):
    n_pages = page_table.shape[1]
    page = cache_kv.shape[2]
    pages = _pick(n_pages, (32, 16, 8, 4, 2))
    n = bs * seq
    rows = MLA_HEADS * seq

    def page_spec(shape, i):
        return pl.BlockSpec((1, 1) + shape, lambda b, j, pt: (layer, pt[b, j * pages + i], 0, 0))

    in_specs = [
        pl.BlockSpec((seq, MLA_HEADS * KV_LORA), lambda b, j, pt: (b, 0)),
        pl.BlockSpec((seq, MLA_HEADS * HEAD_PAD), lambda b, j, pt: (b, 0)),
        pl.BlockSpec((seq, KV_LORA), lambda b, j, pt: (b, 0)),
        pl.BlockSpec((seq, QK_ROPE), lambda b, j, pt: (b, 0)),
    ]
    in_specs += [page_spec((page, KV_LORA), i) for i in range(pages)]
    in_specs += [page_spec((QK_ROPE, page), i) for i in range(pages)]
    grid_spec = pltpu.PrefetchScalarGridSpec(
        num_scalar_prefetch=1,
        grid=(bs, n_pages // pages),
        in_specs=in_specs,
        out_specs=pl.BlockSpec((seq, MLA_HEADS * KV_LORA), lambda b, j, pt: (b, 0)),
        scratch_shapes=[pltpu.VMEM((rows, KV_LORA), F32), pltpu.VMEM((rows, HEAD_PAD), F32),
                        pltpu.VMEM((pages * page, KV_LORA), BF16), pltpu.VMEM((QK_ROPE, pages * page), BF16),
                        pltpu.VMEM((rows, LANES), F32), pltpu.VMEM((rows, LANES), F32),
                        pltpu.VMEM((rows, KV_LORA), F32)],
    )
    return pl.pallas_call(
        functools.partial(_decode_attn_kernel, pages=pages),
        grid_spec=grid_spec,
        out_shape=jax.ShapeDtypeStruct((n, MLA_HEADS * KV_LORA), F32),
        compiler_params=_cparams(("parallel", "arbitrary")),
        name="decode_attention",
    )(page_table, qa, q, ckv, kr, *([cache_kv] * pages), *([cache_kr_t] * pages))


def _merge_kernel(x_ref, ys_ref, ym_ref, gs_ref, gm_ref, ws_ref, wm_ref, wo_ref, o_ref):
    ms = _dot(ys_ref[...].astype(BF16), ws_ref[...])
    mm = _dot(ym_ref[...].astype(BF16), wm_ref[...])
    merged = jax.nn.sigmoid(gs_ref[...].astype(F32)) * ms + jax.nn.sigmoid(gm_ref[...].astype(F32)) * mm
    o_ref[...] = x_ref[...] + _dot(merged.astype(BF16), wo_ref[...])


def merge(x, y_ssm, y_mla, proj, w):
    n, d = x.shape
    tm = _pick(n, (512, 256, 128))
    full = lambda a: _resident(a.shape)
    return pl.pallas_call(
        _merge_kernel,
        grid=(n // tm,),
        in_specs=[
            pl.BlockSpec((tm, d), lambda i: (i, 0)),
            pl.BlockSpec((tm, D_INNER), lambda i: (i, 0)),
            pl.BlockSpec((tm, MLA_HEADS * V_HEAD), lambda i: (i, 0)),
            pl.BlockSpec((tm, d), lambda i: (i, C_GS // D_MODEL)),
            pl.BlockSpec((tm, d), lambda i: (i, C_GM // D_MODEL)),
            full(w["w_br_ssm"]), full(w["w_br_mla"]), full(w["w_out"]),
        ],
        out_specs=pl.BlockSpec((tm, d), lambda i: (i, 0)),
        out_shape=jax.ShapeDtypeStruct((n, d), F32),
        compiler_params=_cparams(("parallel",)),
        name="merge",
    )(x, y_ssm, y_mla, proj, proj, w["w_br_ssm"], w["w_br_mla"], w["w_out"])


def _ple_kernel(x_ref, p_ref, g_ref, wg_ref, wp_ref, fg_ref, o_ref, *, final):
    x = x_ref[...]
    gate = jax.nn.sigmoid(_dot(_rms(x, g_ref[...]).astype(BF16), wg_ref[...]))
    x = x + gate * _dot(p_ref[...].astype(BF16), wp_ref[...])
    if final:
        x = _rms(x, fg_ref[...])
    o_ref[...] = x


def ple(x, p, w, final_norm, final):
    n, d = x.shape
    tm = _pick(n, (512, 256, 128))
    full = lambda a: pl.BlockSpec(a.shape, lambda i: (0, 0))
    return pl.pallas_call(
        functools.partial(_ple_kernel, final=final),
        grid=(n // tm,),
        in_specs=[
            pl.BlockSpec((tm, d), lambda i: (i, 0)),
            pl.BlockSpec((tm, PLE_DIM), lambda i: (i, 0)),
            pl.BlockSpec((1, d), lambda i: (0, 0)),
            full(w["w_ple_gate"]), full(w["w_ple_proj"]),
            pl.BlockSpec((1, d), lambda i: (0, 0)),
        ],
        out_specs=pl.BlockSpec((tm, d), lambda i: (i, 0)),
        out_shape=jax.ShapeDtypeStruct((n, d), F32),
        compiler_params=_cparams(("parallel",)),
        name="ple",
    )(x, p, w["ple_norm"], w["w_ple_gate"], w["w_ple_proj"], final_norm)


def _rope_tables(pos, rows):
    half = QK_ROPE // 2
    inv = ROPE_BASE ** (-jnp.arange(half, dtype=F32) / half)
    ang = pos.astype(F32)[:, None] * inv[None, :]
    cos, sin = jnp.cos(ang), jnp.sin(ang)
    z = lambda w_: jnp.zeros((pos.shape[0], w_), F32)
    cos_t = jnp.concatenate([jnp.ones((pos.shape[0], QK_NOPE), F32), cos, cos, z(32)], axis=1)
    sin_hi = jnp.concatenate([z(QK_NOPE + half), sin, z(32)], axis=1)
    sin_lo = jnp.concatenate([z(QK_NOPE), -sin, z(half + 32)], axis=1)
    reps = max(1, rows // pos.shape[0])
    return tuple(jnp.tile(t, (reps, 1)) for t in (cos_t, sin_hi, sin_lo))


def _layer_weights(i, p):
    row = lambda v: v.reshape(1, -1).astype(F32)
    w_in = p["w_in"][i]
    off_z, off_xbc, off_dt, off_q, off_kv = 2048, 5120, 5152, 5664, 5952
    z = w_in[:, :off_z]
    xbc = w_in[:, off_z:off_xbc]
    dt = w_in[:, off_xbc:off_dt]
    q_lat = w_in[:, off_dt:off_q]
    kv_lat = w_in[:, off_q:off_q + KV_LORA]
    k_rope = w_in[:, off_q + KV_LORA:off_kv]
    gates = w_in[:, off_kv:]
    w_main = jnp.concatenate([z, xbc, gates, q_lat], axis=1).astype(BF16)
    w_main_x = jnp.concatenate([w_main, jnp.zeros((D_MODEL, C_DTX - MAIN_DIM), BF16),
                                jnp.repeat(dt, SSM_HEAD_DIM, axis=1).astype(BF16)], axis=1)
    zeros32 = jnp.zeros((D_MODEL, 32), F32)
    small = jnp.concatenate([kv_lat, dt, zeros32, k_rope, zeros32], axis=1)

    lane_pad = lambda v: jnp.pad(v.astype(F32), (0, LANES - v.shape[0])).reshape(1, LANES)
    a_neg = -jnp.exp(p["a_log"][i].astype(F32))

    w_qb = p["w_qb"][i].reshape(Q_LORA, MLA_HEADS, QK_NOPE + QK_ROPE)
    w_qb_pad = jnp.pad(w_qb, ((0, 0), (0, 0), (0, HEAD_PAD - QK_NOPE - QK_ROPE)))
    w_uk = p["w_uk"][i]
    wk_nope = jnp.pad(w_uk, ((0, 0), (0, 0), (0, HEAD_PAD - QK_NOPE))).reshape(KV_LORA, -1)
    place = jnp.zeros((LANES, MLA_HEADS, HEAD_PAD), F32)
    j = jnp.arange(QK_ROPE)
    place = place.at[S_ROPE - KV_LORA + j, :, QK_NOPE + j].set(1.0).reshape(LANES, -1)
    w_abs = jnp.pad(jnp.transpose(w_uk, (1, 2, 0)), ((0, 0), (0, HEAD_PAD - QK_NOPE), (0, 0)))
    w_uv = p["w_uv"][i]
    eye = jnp.eye(MLA_HEADS, dtype=F32)
    w_uv_bd = (jnp.transpose(w_uv, (1, 0, 2))[:, :, None, :] * eye[:, None, :, None]).reshape(
        MLA_HEADS * KV_LORA, MLA_HEADS * V_HEAD)

    return {
        "ffn1_norm": row(p["ffn1_norm"][i]), "ffn1_w_gu": p["ffn1_w_gu"][i].astype(BF16),
        "ffn1_w_down": p["ffn1_w_down"][i].astype(BF16),
        "ffn2_norm": row(p["ffn2_norm"][i]), "ffn2_w_gu": p["ffn2_w_gu"][i].astype(BF16),
        "ffn2_w_down": p["ffn2_w_down"][i].astype(BF16),
        "mix_norm": row(p["mix_norm"][i]),
        "w_main": w_main, "w_main_x": w_main_x,
        "w_small": small.astype(BF16),
        "conv_w": p["conv_w"][i].astype(F32), "conv_b": row(p["conv_b"][i]),
        "dt_bias_l": lane_pad(p["dt_bias"][i]), "a_neg_l": lane_pad(a_neg),
        "dt_bias_x": row(jnp.repeat(p["dt_bias"][i], SSM_HEAD_DIM)),
        "a_neg_x": row(jnp.repeat(a_neg, SSM_HEAD_DIM)),
        "d_skip_x": row(jnp.repeat(p["d_skip"][i], SSM_HEAD_DIM)),
        "ssm_norm": row(p["ssm_norm"][i]),
        "q_norm": row(p["q_norm"][i]), "kv_norm": row(p["kv_norm"][i]),
        "w_qb_pad": w_qb_pad.reshape(Q_LORA, -1).astype(BF16),
        "wk_full": jnp.concatenate([wk_nope, place], axis=0).astype(BF16),
        "w_uv_t": w_uv.reshape(KV_LORA, -1).T.astype(BF16),
        "w_abs": w_abs.astype(BF16),
        "w_uv_bd": w_uv_bd.astype(BF16),
        "w_br_ssm": p["w_br_ssm"][i].astype(BF16), "w_br_mla": p["w_br_mla"][i].astype(BF16),
        "w_out": p["w_out"][i].astype(BF16),
        "ple_norm": row(p["ple_norm"][i]), "w_ple_gate": p["w_ple_gate"][i].astype(BF16),
        "w_ple_proj": p["w_ple_proj"][i].astype(BF16),
    }


def _trunk_layer(x, p_l, tabs, conv_prev, ssm_prev, w, bs, seq, final_norm, final, paged):
    x = ffn_half_step(x, w["ffn1_norm"], w["ffn1_w_gu"], w["ffn1_w_down"])
    proj = norm_matmul(x, w["mix_norm"], w["w_main"] if paged is None else w["w_main_x"], F32)
    small = norm_matmul(x, w["mix_norm"], w["w_small"], F32)
    if paged is None:
        y_ssm, new_conv, new_ssm = ssm_prefill(proj, small, conv_prev, ssm_prev, w, bs, seq)
        c_kv, k_rope, kfull, vt = kv_project(small, w, tabs, True)
        q = q_project(proj, w, tabs, False)
        y_mla = prefill_attention(q, kfull, vt, bs, seq)
    else:
        cache_kv, cache_kr_t, page_table, layer, new_ssm_stack = paged
        y_ssm, new_conv, new_ssm = ssm_step(proj, conv_prev, ssm_prev, new_ssm_stack, layer, w, bs, seq)
        c_kv, k_rope = kv_project(small, w, tabs, False)
        q, qa = q_project(proj, w, tabs, True)
        o_lat = decode_attention(qa, q, c_kv, k_rope, cache_kv, cache_kr_t, page_table, layer, bs, seq)
        y_mla = norm_matmul(o_lat, None, w["w_uv_bd"], BF16, norm=False, tm_prefs=(256, 128))
    x = merge(x, y_ssm, y_mla, proj, w)
    x = ffn_half_step(x, w["ffn2_norm"], w["ffn2_w_gu"], w["ffn2_w_down"])
    x = ple(x, p_l, w, final_norm, final)
    return x, c_kv, k_rope, new_conv, new_ssm


def kernel(x_prompt, x_sample, cache_kv, cache_kr, state_conv, state_ssm, page_table, p_prompt, p_sample,
           ffn1_norm, ffn1_w_gu, ffn1_w_down, mix_norm, w_in, conv_w, conv_b, dt_bias, a_log, d_skip,
           ssm_norm, q_norm, w_qb, kv_norm, w_uk, w_uv, w_br_ssm, w_br_mla, w_out,
           ffn2_norm, ffn2_w_gu, ffn2_w_down, ple_norm, w_ple_gate, w_ple_proj, final_norm):
    params = dict(ffn1_norm=ffn1_norm, ffn1_w_gu=ffn1_w_gu, ffn1_w_down=ffn1_w_down, mix_norm=mix_norm,
                  w_in=w_in, conv_w=conv_w, conv_b=conv_b, dt_bias=dt_bias, a_log=a_log, d_skip=d_skip,
                  ssm_norm=ssm_norm, q_norm=q_norm, w_qb=w_qb, kv_norm=kv_norm, w_uk=w_uk, w_uv=w_uv,
                  w_br_ssm=w_br_ssm, w_br_mla=w_br_mla, w_out=w_out, ffn2_norm=ffn2_norm,
                  ffn2_w_gu=ffn2_w_gu, ffn2_w_down=ffn2_w_down, ple_norm=ple_norm,
                  w_ple_gate=w_ple_gate, w_ple_proj=w_ple_proj)
    depth = w_in.shape[0]
    bp, sp, d = x_prompt.shape
    bsm, ss, _ = x_sample.shape
    past_len = page_table.shape[1] * cache_kv.shape[2]
    np_, ns = bp * sp, bsm * ss

    tabs_p = _rope_tables(jnp.arange(sp), sp)
    tabs_s = _rope_tables(past_len + jnp.arange(ss), _pick(ns, (256, 128)))
    conv0 = jnp.zeros((bp, D_CONV - 1, CONV_DIM), F32)
    ssm0 = jnp.zeros((bp, SSM_HEADS, SSM_HEAD_DIM, SSM_STATE), F32)
    fnorm = final_norm.reshape(1, -1).astype(F32)

    cache_kr_t = jnp.swapaxes(cache_kr, 2, 3)

    hp = x_prompt.reshape(np_, d)
    hs = x_sample.reshape(ns, d)
    outs_p, outs_s = [], []
    new_ssm_stack = None
    for i in range(depth):
        last = i == depth - 1
        w = _layer_weights(i, params)
        hp, *rest_p = _trunk_layer(hp, p_prompt[i].reshape(np_, PLE_DIM), tabs_p, conv0, ssm0, w,
                                   bp, sp, fnorm, last, None)
        hs, *rest_s = _trunk_layer(hs, p_sample[i].reshape(ns, PLE_DIM), tabs_s, state_conv,
                                   state_ssm, w, bsm, ss, fnorm, last,
                                   (cache_kv, cache_kr_t, page_table, i, new_ssm_stack))
        new_ssm_stack = rest_s[3]
        outs_p.append(rest_p)
        outs_s.append(rest_s)

    stack = lambda outs, k, shape: jnp.stack([o[k] for o in outs]).reshape((depth,) + shape)
    return (
        hp.reshape(bp, sp, d), hs.reshape(bsm, ss, d),
        stack(outs_p, 0, (bp, sp, KV_LORA)), stack(outs_p, 1, (bp, sp, QK_ROPE)),
        stack(outs_p, 2, (bp, D_CONV - 1, CONV_DIM)),
        stack(outs_p, 3, (bp, SSM_HEADS, SSM_HEAD_DIM, SSM_STATE)),
        stack(outs_s, 0, (bsm, ss, KV_LORA)), stack(outs_s, 1, (bsm, ss, QK_ROPE)),
        stack(outs_s, 2, (bsm, D_CONV - 1, CONV_DIM)),
        new_ssm_stack,
    )
```

```python
import functools
import math

import jax
import jax.numpy as jnp
from jax import lax
from jax.experimental import pallas as pl
from jax.experimental.pallas import tpu as pltpu

F32 = jnp.float32
BF16 = jnp.bfloat16

D_MODEL = 1024
D_FF = 2816
D_INNER = 2048
SSM_HEAD_DIM = 64
SSM_HEADS = 32
SSM_GROUPS = 4
HEADS_PER_GROUP = SSM_HEADS // SSM_GROUPS
SSM_STATE = 128
GROUP_CH = D_INNER // SSM_GROUPS
D_CONV = 4
CONV_DIM = D_INNER + 2 * SSM_GROUPS * SSM_STATE
SSD_CHUNK = 128
MLA_HEADS = 16
Q_LORA = 512
KV_LORA = 256
QK_NOPE = 64
QK_ROPE = 32
V_HEAD = 64
ROPE_BASE = 10000.0
ATTN_SCALE = (QK_NOPE + QK_ROPE) ** -0.5
SCORE_SCALE = ATTN_SCALE * math.log2(math.e)
PLE_DIM = 256
EPS = 1e-6
LANES = 128
HEAD_PAD = 128
DECODE_CHUNK_PAGES = 4
NEG_BIG = -1e30

C_Z, C_XS, C_BC, C_GS, C_GM, C_Q = 0, 2048, 4096, 5120, 6144, 7168
MAIN_DIM = 7680
C_DTX = 8192
MAIN_DIM_X = C_DTX + D_INNER
SMALL_DIM = 384
S_DT, S_ROPE = 256, 320

VMEM_LIMIT = 56 * 1024 * 1024


def _cparams(sem):
    return pltpu.CompilerParams(dimension_semantics=sem, vmem_limit_bytes=VMEM_LIMIT)


def _pick(n, prefs):
    for p in prefs:
        if n % p == 0:
            return p
    return n


def _dot(a, b):
    return jnp.dot(a, b, preferred_element_type=F32)


def _dot_nt(a, b):
    return lax.dot_general(a, b, (((1,), (1,)), ((), ())), preferred_element_type=F32)


def _silu(x):
    return x * jax.nn.sigmoid(x)


def _softplus(x):
    return jnp.maximum(x, 0.0) + jnp.log1p(jnp.exp(-jnp.abs(x)))


def _rms(x, g):
    return x * lax.rsqrt(jnp.mean(x * x, axis=-1, keepdims=True) + EPS) * g


def _nm_kernel(x_ref, g_ref, w_ref, o_ref, u_sc, *, norm):
    @pl.when(pl.program_id(1) == 0)
    def _():
        x = x_ref[...].astype(F32)
        if norm:
            x = _rms(x, g_ref[...])
        u_sc[...] = x.astype(BF16)

    o_ref[...] = _dot(u_sc[...], w_ref[...]).astype(o_ref.dtype)


def norm_matmul(x, g, w, out_dtype, *, norm=True, col_block=0, tm_prefs=(1024, 512, 256, 128)):
    n = x.shape[0]
    k, nout = w.shape
    tm = _pick(n, tm_prefs)
    tn = _pick(nout, (1280, 1024, 512, 384, 256, 128))
    if not norm:
        g = jnp.ones((1, k), F32)
    return pl.pallas_call(
        functools.partial(_nm_kernel, norm=norm),
        grid=(n // tm, nout // tn),
        in_specs=[
            pl.BlockSpec((tm, k), lambda i, j: (i, col_block)),
            pl.BlockSpec((1, k), lambda i, j: (0, 0)),
            pl.BlockSpec((k, tn), lambda i, j: (0, j)),
        ],
        out_specs=pl.BlockSpec((tm, tn), lambda i, j: (i, j)),
        out_shape=jax.ShapeDtypeStruct((n, nout), out_dtype),
        scratch_shapes=[pltpu.VMEM((tm, k), BF16)],
        compiler_params=_cparams(("parallel", "arbitrary")),
        name="norm_matmul",
    )(x, g, w)


def _ffn_kernel(x_ref, g_ref, wgu_ref, wd_ref, o_ref):
    x = x_ref[...]
    dff = wd_ref.shape[0]
    u = _rms(x, g_ref[...]).astype(BF16)
    a = _dot(u, wgu_ref[:, :dff])
    b = _dot(u, wgu_ref[:, dff:])
    h = (_silu(a) * b).astype(BF16)
    o_ref[...] = x + 0.5 * _dot(h, wd_ref[...])


def _resident(shape):
    return pl.BlockSpec(shape, lambda *_: (0,) * len(shape), pipeline_mode=pl.Buffered(1))


def ffn_half_step(x, g, w_gu, w_down):
    n, d = x.shape
    tm = _pick(n, (512, 256, 128))
    return pl.pallas_call(
        _ffn_kernel,
        grid=(n // tm,),
        in_specs=[
            pl.BlockSpec((tm, d), lambda i: (i, 0)),
            pl.BlockSpec((1, d), lambda i: (0, 0)),
            _resident(w_gu.shape),
            _resident(w_down.shape),
        ],
        out_specs=pl.BlockSpec((tm, d), lambda i: (i, 0)),
        out_shape=jax.ShapeDtypeStruct((n, d), F32),
        compiler_params=_cparams(("parallel",)),
        name="ffn_half_step",
    )(x, g, w_gu, w_down)


def _causal_conv(xpad_sc, rows, w_ref, b_ref):
    w = w_ref[...]
    acc = xpad_sc[5:5 + rows, :] * w[0:1, :]
    for k in range(1, D_CONV):
        acc = acc + xpad_sc[5 + k:5 + k + rows, :] * w[k:k + 1, :]
    return _silu(acc + b_ref[...])


def _gate_and_group_norm(y, z, norm_w):
    y = y * _silu(z)
    parts = []
    for g in range(SSM_GROUPS):
        yg = y[:, g * GROUP_CH:(g + 1) * GROUP_CH]
        parts.append(yg * lax.rsqrt(jnp.mean(yg * yg, axis=-1, keepdims=True) + EPS))
    return jnp.concatenate(parts, axis=1) * norm_w


def _cumsum_rows(a):
    rows = a.shape[0]
    row = lax.broadcasted_iota(jnp.int32, a.shape, 0)
    s = 1
    while s < rows:
        a = a + jnp.where(row >= s, pltpu.roll(a, s, axis=0), 0.0)
        s *= 2
    return a


def _ssm_prefill_kernel(z_ref, xs_ref, bc_ref, dt_ref, cprev_ref, sprev_ref, convw_ref, convb_ref,
                        dtb_ref, aneg_ref, dskip_ref, norm_ref,
                        y_ref, nconv_ref, nssm_ref, xpad_sc, h_sc):
    c = pl.program_id(1)
    t = SSD_CHUNK
    hp = SSM_HEAD_DIM

    @pl.when(c == 0)
    def _():
        xpad_sc[5:8, :] = cprev_ref[0]
        h_sc[...] = sprev_ref[0].reshape(D_INNER, SSM_STATE)

    xpad_sc[8:8 + t, 0:D_INNER] = xs_ref[...]
    xpad_sc[8:8 + t, D_INNER:CONV_DIM] = bc_ref[...]
    conv = _causal_conv(xpad_sc, t, convw_ref, convb_ref)
    tail = xpad_sc[8 + t - 3:8 + t, :]
    xpad_sc[5:8, :] = tail

    xs = conv[:, :D_INNER]
    gs = SSM_GROUPS * SSM_STATE
    bm = conv[:, D_INNER:D_INNER + gs]
    cm = conv[:, D_INNER + gs:]

    dt = _softplus(dt_ref[...] + dtb_ref[...])
    acum = _cumsum_rows(dt * aneg_ref[...])
    alast = acum[t - 1:t, :]
    acum_t = acum.T
    dt_t = dt.T
    wst_t = (dt * jnp.exp(alast - acum)).T
    xs_t = xs.T

    tri = (lax.broadcasted_iota(jnp.int32, (t, t), 0) >= lax.broadcasted_iota(jnp.int32, (t, t), 1))
    low_half = lax.broadcasted_iota(jnp.int32, (t, 2 * hp), 1) < hp

    y_parts = []
    for g in range(SSM_GROUPS):
        bg = bm[:, g * SSM_STATE:(g + 1) * SSM_STATE]
        cg = cm[:, g * SSM_STATE:(g + 1) * SSM_STATE]
        cb = _dot_nt(cg.astype(BF16), bg.astype(BF16))
        for pair in range(HEADS_PER_GROUP // 2):
            h0 = g * HEADS_PER_GROUP + 2 * pair
            lhs = []
            for h in (h0, h0 + 1):
                colb = jnp.broadcast_to(acum[:, h:h + 1], (t, t))
                decay = jnp.where(tri, jnp.exp(colb - acum_t[h:h + 1, :]), 0.0)
                m = cb * decay * dt_t[h:h + 1, :]
                ce = cg * jnp.exp(colb)
                lhs.append(jnp.concatenate([m, ce], axis=1))
            lhs = jnp.concatenate(lhs, axis=0).astype(BF16)
            r0 = h0 * hp
            rhs_t = jnp.concatenate([xs_t[r0:r0 + 2 * hp, :], h_sc[r0:r0 + 2 * hp, :]],
                                    axis=1).astype(BF16)
            out = _dot_nt(lhs, rhs_t)
            y_parts.append(jnp.where(low_half, out[:t, :], out[t:, :]))
    y = jnp.concatenate(y_parts, axis=1) + dskip_ref[...] * xs
    y_ref[...] = _gate_and_group_norm(y, z_ref[...], norm_ref[...]).astype(y_ref.dtype)

    elast = jnp.exp(alast)
    for g in range(SSM_GROUPS):
        bg = bm[:, g * SSM_STATE:(g + 1) * SSM_STATE].astype(BF16)
        xw = []
        for hh in range(HEADS_PER_GROUP):
            h = g * HEADS_PER_GROUP + hh
            xw.append(xs_t[h * hp:(h + 1) * hp, :] * wst_t[h:h + 1, :])
        dh = _dot(jnp.concatenate(xw, axis=0).astype(BF16), bg)
        for hh in range(HEADS_PER_GROUP):
            h = g * HEADS_PER_GROUP + hh
            rows = slice(h * hp, (h + 1) * hp)
            h_sc[rows, :] = h_sc[rows, :] * elast[:, h:h + 1] + dh[hh * hp:(hh + 1) * hp, :]

    @pl.when(c == pl.num_programs(1) - 1)
    def _():
        nconv_ref[0] = tail
        nssm_ref[0] = h_sc[...].reshape(SSM_HEADS, SSM_HEAD_DIM, SSM_STATE)


def ssm_prefill(proj, small, conv_prev, ssm_prev, w, bs, seq):
    t = SSD_CHUNK
    nc = seq // t
    n = bs * seq
    row = lambda b, c: b * nc + c
    vec = lambda width: pl.BlockSpec((1, width), lambda b, c: (0, 0))
    return pl.pallas_call(
        _ssm_prefill_kernel,
        grid=(bs, nc),
        in_specs=[
            pl.BlockSpec((t, D_INNER), lambda b, c: (row(b, c), C_Z // D_INNER)),
            pl.BlockSpec((t, D_INNER), lambda b, c: (row(b, c), C_XS // D_INNER)),
            pl.BlockSpec((t, 1024), lambda b, c: (row(b, c), C_BC // 1024)),
            pl.BlockSpec((t, LANES), lambda b, c: (row(b, c), S_DT // LANES)),
            pl.BlockSpec((1, D_CONV - 1, CONV_DIM), lambda b, c: (b, 0, 0)),
            pl.BlockSpec((1, SSM_HEADS, SSM_HEAD_DIM, SSM_STATE), lambda b, c: (b, 0, 0, 0)),
            pl.BlockSpec((D_CONV, CONV_DIM), lambda b, c: (0, 0)),
            vec(CONV_DIM), vec(LANES), vec(LANES), vec(D_INNER), vec(D_INNER),
        ],
        out_specs=[
            pl.BlockSpec((t, D_INNER), lambda b, c: (row(b, c), 0)),
            pl.BlockSpec((1, D_CONV - 1, CONV_DIM), lambda b, c: (b, 0, 0)),
            pl.BlockSpec((1, SSM_HEADS, SSM_HEAD_DIM, SSM_STATE), lambda b, c: (b, 0, 0, 0)),
        ],
        out_shape=[
            jax.ShapeDtypeStruct((n, D_INNER), F32),
            jax.ShapeDtypeStruct((bs, D_CONV - 1, CONV_DIM), F32),
            jax.ShapeDtypeStruct((bs, SSM_HEADS, SSM_HEAD_DIM, SSM_STATE), F32),
        ],
        scratch_shapes=[pltpu.VMEM((t + 8, CONV_DIM), F32), pltpu.VMEM((D_INNER, SSM_STATE), F32)],
        compiler_params=_cparams(("parallel", "arbitrary")),
        name="ssm_prefill",
    )(proj, proj, proj, small, conv_prev, ssm_prev, w["conv_w"], w["conv_b"], w["dt_bias_l"],
      w["a_neg_l"], w["d_skip_x"], w["ssm_norm"])


def _ssm_step_kernel(z_ref, xs_ref, bc_ref, dtx_ref, cprev_ref, sprev_ref, convw_ref, convb_ref,
                     dtbx_ref, anegx_ref, dskip_ref, norm_ref, *rest):
    y_ref, nconv_ref, nssm_ref, xpad_sc, xw_sc, bpad_sc = rest[-6:]
    seq = xs_ref.shape[0]
    hp = SSM_HEAD_DIM
    gs = SSM_GROUPS * SSM_STATE

    @pl.when(pl.program_id(0) == 0)
    def _():
        xw_sc[...] = jnp.zeros_like(xw_sc)
        bpad_sc[...] = jnp.zeros_like(bpad_sc)

    xpad_sc[5:8, :] = cprev_ref[0, 0]
    xpad_sc[8:8 + seq, 0:D_INNER] = xs_ref[...]
    xpad_sc[8:8 + seq, D_INNER:CONV_DIM] = bc_ref[...]
    conv = _causal_conv(xpad_sc, seq, convw_ref, convb_ref)
    nconv_ref[0] = xpad_sc[8 + seq - 3:8 + seq, :]

    xs = conv[:, :D_INNER]
    bm = conv[:, D_INNER:D_INNER + gs]
    cm = conv[:, D_INNER + gs:]

    dtx = _softplus(dtx_ref[...] + dtbx_ref[...])
    acum = _cumsum_rows(dtx * anegx_ref[...])
    alast = acum[seq - 1:seq, :]
    xd = xs * dtx

    c_all = jnp.concatenate([cm[:, g * SSM_STATE:(g + 1) * SSM_STATE] for g in range(SSM_GROUPS)],
                            axis=0).astype(BF16)
    y_off = []
    for g in range(SSM_GROUPS):
        hg = sprev_ref[0, 0, g * HEADS_PER_GROUP:(g + 1) * HEADS_PER_GROUP].reshape(GROUP_CH, SSM_STATE)
        y_off.append(_dot_nt(c_all, hg.astype(BF16))[g * seq:(g + 1) * seq, :])
    y = jnp.concatenate(y_off, axis=1) * jnp.exp(acum)

    ones = jnp.ones((SSM_STATE, GROUP_CH), BF16)
    cbx = []
    for g in range(SSM_GROUPS):
        bg = bm[:, g * SSM_STATE:(g + 1) * SSM_STATE]
        cg = cm[:, g * SSM_STATE:(g + 1) * SSM_STATE]
        prod = jnp.concatenate([cg * bg[s:s + 1, :] for s in range(seq)], axis=0)
        hi = prod.astype(BF16)
        lo = (prod - hi.astype(F32)).astype(BF16)
        cbx.append(_dot(hi, ones) + _dot(lo, ones))
    row = lax.broadcasted_iota(jnp.int32, (seq, D_INNER), 0)
    for s in range(seq):
        cb_s = jnp.concatenate([cbx[g][s * seq:(s + 1) * seq, :] for g in range(SSM_GROUPS)], axis=1)
        decay = jnp.where(row >= s, jnp.exp(acum - acum[s:s + 1, :]), 0.0)
        y = y + cb_s * decay * xd[s:s + 1, :]

    y = y + dskip_ref[...] * xs
    y_ref[...] = _gate_and_group_norm(y, z_ref[...], norm_ref[...]).astype(y_ref.dtype)

    xw_sc[0:seq, :] = xd * jnp.exp(alast - acum)
    bpad_sc[0:seq, :] = bm
    xw_t = xw_sc[...].T
    elast = jnp.exp(alast)
    for g in range(SSM_GROUPS):
        dh = _dot(xw_t[g * GROUP_CH:(g + 1) * GROUP_CH, :].astype(BF16),
                  bpad_sc[:, g * SSM_STATE:(g + 1) * SSM_STATE].astype(BF16))
        for hh in range(HEADS_PER_GROUP):
            h = g * HEADS_PER_GROUP + hh
            nssm_ref[0, 0, h] = (sprev_ref[0, 0, h] * elast[:, h * hp:h * hp + 1]
                                 + dh[hh * hp:(hh + 1) * hp, :])


def ssm_step(proj, state_conv, state_ssm, new_ssm_stack, layer, w, bs, seq):
    n = bs * seq
    vec = lambda width: pl.BlockSpec((1, width), lambda b: (0, 0))
    state_block = (1, 1, SSM_HEADS, SSM_HEAD_DIM, SSM_STATE)
    in_specs = [
        pl.BlockSpec((seq, D_INNER), lambda b: (b, C_Z // D_INNER)),
        pl.BlockSpec((seq, D_INNER), lambda b: (b, C_XS // D_INNER)),
        pl.BlockSpec((seq, 1024), lambda b: (b, C_BC // 1024)),
        pl.BlockSpec((seq, D_INNER), lambda b: (b, C_DTX // D_INNER)),
        pl.BlockSpec((1, 1, D_CONV - 1, CONV_DIM), lambda b: (layer, b, 0, 0)),
        pl.BlockSpec(state_block, lambda b: (layer, b, 0, 0, 0)),
        pl.BlockSpec((D_CONV, CONV_DIM), lambda b: (0, 0)),
        vec(CONV_DIM), vec(D_INNER), vec(D_INNER), vec(D_INNER), vec(D_INNER),
    ]
    args = [proj, proj, proj, proj, state_conv, state_ssm, w["conv_w"], w["conv_b"], w["dt_bias_x"],
            w["a_neg_x"], w["d_skip_x"], w["ssm_norm"]]
    aliases = {}
    if new_ssm_stack is not None:
        in_specs.append(pl.BlockSpec(memory_space=pl.ANY))
        args.append(new_ssm_stack)
        aliases = {len(args) - 1: 2}
    return pl.pallas_call(
        _ssm_step_kernel,
        grid=(bs,),
        in_specs=in_specs,
        out_specs=[
            pl.BlockSpec((seq, D_INNER), lambda b: (b, 0)),
            pl.BlockSpec((1, D_CONV - 1, CONV_DIM), lambda b: (b, 0, 0)),
            pl.BlockSpec(state_block, lambda b: (layer, b, 0, 0, 0)),
        ],
        out_shape=[
            jax.ShapeDtypeStruct((n, D_INNER), F32),
            jax.ShapeDtypeStruct((bs, D_CONV - 1, CONV_DIM), F32),
            jax.ShapeDtypeStruct(state_ssm.shape, F32),
        ],
        scratch_shapes=[pltpu.VMEM((seq + 8, CONV_DIM), F32), pltpu.VMEM((LANES, D_INNER), F32),
                        pltpu.VMEM((LANES, SSM_GROUPS * SSM_STATE), F32)],
        input_output_aliases=aliases,
        compiler_params=_cparams(("arbitrary",)),
        name="ssm_step",
    )(*args)


def _rope(x, cos, sin_hi, sin_lo):
    return x * cos + pltpu.roll(x, 16, axis=1) * sin_hi + pltpu.roll(x, x.shape[1] - 16, axis=1) * sin_lo


def _kv_kernel(s_ref, g_ref, cos_ref, shi_ref, slo_ref, wk_ref, wv_ref,
               ckv_ref, kr_ref, *kv_out, with_kv):
    blk = s_ref[...]
    ckv = _rms(blk[:, :KV_LORA], g_ref[...])
    ckv_ref[...] = ckv
    tail = _rope(blk[:, KV_LORA:], cos_ref[...], shi_ref[...], slo_ref[...])
    kr_ref[...] = tail[:, S_ROPE - KV_LORA:S_ROPE - KV_LORA + QK_ROPE]
    if with_kv:
        kfull_ref, vt_ref = kv_out
        ckv16 = ckv.astype(BF16)
        lhs = jnp.concatenate([ckv16, tail.astype(BF16)], axis=1)
        kfull_ref[...] = _dot(lhs, wk_ref[...]).astype(kfull_ref.dtype)
        vt_ref[0] = _dot_nt(wv_ref[...], ckv16).astype(vt_ref.dtype)


def kv_project(small, w, tabs, with_kv):
    n = small.shape[0]
    tm = min(_pick(n, (512, 256, 128)), tabs[0].shape[0])
    ntab = tabs[0].shape[0] // tm
    tab = pl.BlockSpec((tm, LANES), lambda i: (i % ntab, 0))
    out_specs = [pl.BlockSpec((tm, KV_LORA), lambda i: (i, 0)), pl.BlockSpec((tm, QK_ROPE), lambda i: (i, 0))]
    out_shape = [jax.ShapeDtypeStruct((n, KV_LORA), F32), jax.ShapeDtypeStruct((n, QK_ROPE), F32)]
    if with_kv:
        out_specs += [pl.BlockSpec((tm, MLA_HEADS * HEAD_PAD), lambda i: (i, 0)),
                      pl.BlockSpec((1, MLA_HEADS * V_HEAD, tm), lambda i: (i, 0, 0))]
        out_shape += [jax.ShapeDtypeStruct((n, MLA_HEADS * HEAD_PAD), BF16),
                      jax.ShapeDtypeStruct((n // tm, MLA_HEADS * V_HEAD, tm), BF16)]
    return pl.pallas_call(
        functools.partial(_kv_kernel, with_kv=with_kv),
        grid=(n // tm,),
        in_specs=[
            pl.BlockSpec((tm, SMALL_DIM), lambda i: (i, 0)),
            pl.BlockSpec((1, KV_LORA), lambda i: (0, 0)),
            tab, tab, tab,
            pl.BlockSpec(w["wk_full"].shape, lambda i: (0, 0)),
            pl.BlockSpec(w["w_uv_t"].shape, lambda i: (0, 0)),
        ],
        out_specs=out_specs,
        out_shape=out_shape,
        compiler_params=_cparams(("parallel",)),
        name="kv_project",
    )(small, w["kv_norm"], *tabs, w["wk_full"], w["w_uv_t"])


def _q_kernel(x_ref, g_ref, cos_ref, shi_ref, slo_ref, wq_ref, *rest, absorb):
    u = _rms(x_ref[...].astype(F32), g_ref[...]).astype(BF16)
    q = _dot(u, wq_ref[...])
    tile = lambda r: jnp.concatenate([r[...]] * MLA_HEADS, axis=1)
    q = _rope(q, tile(cos_ref), tile(shi_ref), tile(slo_ref))
    if absorb:
        wabs_ref, q_ref, qa_ref = rest
        q_ref[...] = q.astype(q_ref.dtype)
        q16 = q.astype(BF16)
        for h in range(MLA_HEADS):
            qa_ref[:, h * KV_LORA:(h + 1) * KV_LORA] = _dot(
                q16[:, h * HEAD_PAD:(h + 1) * HEAD_PAD], wabs_ref[h]).astype(qa_ref.dtype)
    else:
        (q_ref,) = rest
        q_ref[...] = (q * SCORE_SCALE).astype(q_ref.dtype)


def q_project(proj, w, tabs, absorb):
    n = proj.shape[0]
    tm = min(_pick(n, (512, 256, 128)), tabs[0].shape[0])
    ntab = tabs[0].shape[0] // tm
    tab = pl.BlockSpec((tm, LANES), lambda i: (i % ntab, 0))
    qw = MLA_HEADS * HEAD_PAD
    in_specs = [
        pl.BlockSpec((tm, Q_LORA), lambda i: (i, C_Q // Q_LORA)),
        pl.BlockSpec((1, Q_LORA), lambda i: (0, 0)),
        tab, tab, tab,
        pl.BlockSpec((Q_LORA, qw), lambda i: (0, 0)),
    ]
    args = [proj, w["q_norm"], *tabs, w["w_qb_pad"]]
    if absorb:
        in_specs.append(pl.BlockSpec((MLA_HEADS, HEAD_PAD, KV_LORA), lambda i: (0, 0, 0)))
        args.append(w["w_abs"])
        out_specs = [pl.BlockSpec((tm, qw), lambda i: (i, 0)),
                     pl.BlockSpec((tm, MLA_HEADS * KV_LORA), lambda i: (i, 0))]
        out_shape = [jax.ShapeDtypeStruct((n, qw), F32),
                     jax.ShapeDtypeStruct((n, MLA_HEADS * KV_LORA), F32)]
    else:
        out_specs = pl.BlockSpec((tm, qw), lambda i: (i, 0))
        out_shape = jax.ShapeDtypeStruct((n, qw), BF16)
    return pl.pallas_call(
        functools.partial(_q_kernel, absorb=absorb),
        grid=(n // tm,),
        in_specs=in_specs,
        out_specs=out_specs,
        out_shape=out_shape,
        compiler_params=_cparams(("parallel",)),
        name="q_project",
    )(*args)


def _bcast_lanes(x, width):
    return jnp.concatenate([x] * (width // LANES), axis=1)


def _prefill_attn_kernel(q_ref, k_ref, vt_ref, o_ref, m_sc, l_sc, acc_sc):
    qi = pl.program_id(2)
    tq = q_ref.shape[0]
    tk = vt_ref.shape[2]
    nsplit, _, tc = acc_sc.shape
    head0 = lax.broadcasted_iota(jnp.int32, (2 * V_HEAD, tc), 0) < V_HEAD

    m_sc[...] = jnp.full_like(m_sc, NEG_BIG)
    l_sc[...] = jnp.zeros_like(l_sc)
    acc_sc[...] = jnp.zeros_like(acc_sc)

    def update(kis, masked):
        scores = []
        for ki in kis:
            start = pl.multiple_of(ki * tk, tk)
            for c in range(nsplit):
                for h in range(2):
                    qh = q_ref[c * tc:(c + 1) * tc, h * HEAD_PAD:(h + 1) * HEAD_PAD]
                    kh = k_ref[pl.ds(start, tk), h * HEAD_PAD:(h + 1) * HEAD_PAD]
                    scores.append(_dot_nt(kh, qh))
        for n, ki in enumerate(kis):
            vt = vt_ref[ki]
            for c in range(nsplit):
                alphas, pvs = [], []
                for h in range(2):
                    st = c * 2 + h
                    s = scores[n * 2 * nsplit + st]
                    if masked:
                        keep = (lax.broadcasted_iota(jnp.int32, (tk, tc), 0)
                                <= lax.broadcasted_iota(jnp.int32, (tk, tc), 1) + c * tc)
                        s = jnp.where(keep, s, NEG_BIG)
                    m_prev = m_sc[st]
                    m_next = jnp.maximum(m_prev, jnp.max(s, axis=0, keepdims=True))
                    p = jnp.exp2(s - m_next)
                    alpha = jnp.exp2(m_prev - m_next)
                    l_sc[st] = alpha * l_sc[st] + jnp.sum(p, axis=0, keepdims=True)
                    m_sc[st] = m_next
                    alphas.append(alpha)
                    pvs.append(_dot(vt, p.astype(BF16)))
                acc_sc[c] = (jnp.where(head0, alphas[0], alphas[1]) * acc_sc[c]
                             + jnp.where(head0, pvs[0], pvs[1]))

    def body(j, carry):
        update((2 * j, 2 * j + 1), False)
        return carry

    lax.fori_loop(0, qi // 2, body, 0)

    @pl.when(qi % 2 == 1)
    def _():
        update((qi - 1,), False)

    update((qi,), True)
    for c in range(nsplit):
        o_t = acc_sc[c] / jnp.where(head0, l_sc[2 * c], l_sc[2 * c + 1])
        o_ref[c * tc:(c + 1) * tc, :] = o_t.T.astype(o_ref.dtype)


def prefill_attention(q, kfull, vt, bs, seq):
    tk = vt.shape[2]
    tq = tk
    nq = seq // tq
    n = bs * seq
    nsplit = 2 if tq % (2 * LANES) == 0 else 1
    tc = tq // nsplit
    return pl.pallas_call(
        _prefill_attn_kernel,
        grid=(bs, MLA_HEADS // 2, nq),
        in_specs=[
            pl.BlockSpec((tq, 2 * HEAD_PAD), lambda b, hp, qi: (b * nq + qi, hp)),
            pl.BlockSpec((seq, 2 * HEAD_PAD), lambda b, hp, qi: (b, hp)),
            pl.BlockSpec((nq, 2 * V_HEAD, tk), lambda b, hp, qi: (b, hp, 0)),
        ],
        out_specs=pl.BlockSpec((tq, 2 * V_HEAD), lambda b, hp, qi: (b * nq + qi, hp)),
        out_shape=jax.ShapeDtypeStruct((n, MLA_HEADS * V_HEAD), BF16),
        scratch_shapes=[pltpu.VMEM((2 * nsplit, 1, tc), F32), pltpu.VMEM((2 * nsplit, 1, tc), F32),
                        pltpu.VMEM((nsplit, 2 * V_HEAD, tc), F32)],
        compiler_params=_cparams(("parallel", "parallel", "arbitrary")),
        name="prefill_attention",
    )(q, kfull, vt)


def _decode_attn_kernel(pt_ref, qa_ref, q_ref, cnew_ref, rnew_ref, kv_hbm, kr_hbm, o_ref,
                        kv_buf, kr_buf, kv_sem, kr_sem, *, layer, n_pages):
    b = pl.program_id(0)
    nb = pl.num_programs(0)
    seq = qa_ref.shape[0]
    rows = MLA_HEADS * seq
    page = kv_hbm.shape[2]
    slot = b % 2

    def page_copies(seq_idx, dst_slot, i):
        pid = pt_ref[seq_idx, i]
        return (pltpu.make_async_copy(kv_hbm.at[layer, pid],
                                      kv_buf.at[dst_slot, pl.ds(i * page, page), :],
                                      kv_sem.at[dst_slot]),
                pltpu.make_async_copy(kr_hbm.at[layer, pid],
                                      kr_buf.at[dst_slot, :, pl.ds(i * page, page)],
                                      kr_sem.at[dst_slot]))

    def start_fetch(seq_idx, dst_slot):
        for i in range(n_pages):
            for cp in page_copies(seq_idx, dst_slot, i):
                cp.start()

    @pl.when(b == 0)
    def _():
        start_fetch(0, 0)

    @pl.when(b + 1 < nb)
    def _():
        start_fetch(b + 1, 1 - slot)

    qa = jnp.concatenate([qa_ref[:, h * KV_LORA:(h + 1) * KV_LORA] for h in range(MLA_HEADS)],
                         axis=0) * SCORE_SCALE
    qr = jnp.concatenate([q_ref[:, h * HEAD_PAD:(h + 1) * HEAD_PAD] for h in range(MLA_HEADS)],
                         axis=0)[:, QK_NOPE:QK_NOPE + QK_ROPE] * SCORE_SCALE
    qa = qa.astype(BF16)
    qr = qr.astype(BF16)

    for i in range(n_pages):
        for cp in page_copies(b, slot, i):
            cp.wait()

    cp_pages = min(n_pages, DECODE_CHUNK_PAGES)
    ck = cp_pages * page
    chunks, scores = [], []
    for st in range(0, n_pages * page, ck):
        c = kv_buf[slot, pl.ds(st, ck), :].astype(BF16)
        r = kr_buf[slot, :, pl.ds(st, ck)].astype(BF16)
        chunks.append(c)
        scores.append(_dot_nt(qa, c) + _dot(qr, r))
    pad = page - seq
    c_new = jnp.concatenate([cnew_ref[...], jnp.zeros((pad, KV_LORA), F32)], axis=0).astype(BF16)
    r_new = jnp.concatenate([rnew_ref[...], jnp.zeros((pad, QK_ROPE), F32)], axis=0).astype(BF16)
    s_new = _dot_nt(qa, c_new) + _dot_nt(qr, r_new)
    tok = lax.broadcasted_iota(jnp.int32, (rows, page), 0) % seq
    key = lax.broadcasted_iota(jnp.int32, (rows, page), 1)
    chunks.append(c_new)
    scores.append(jnp.where(key <= tok, s_new, NEG_BIG))

    m = jnp.max(scores[0], axis=1, keepdims=True)
    for s in scores[1:]:
        m = jnp.maximum(m, jnp.max(s, axis=1, keepdims=True))
    l = jnp.zeros((rows, 1), F32)
    acc = jnp.zeros((rows, KV_LORA), F32)
    for c, s in zip(chunks, scores):
        p = jnp.exp2(s - m)
        l = l + jnp.sum(p, axis=1, keepdims=True)
        acc = acc + _dot(p.astype(BF16), c)
    o = acc / l
    for h in range(MLA_HEADS):
        o_ref[:, h * KV_LORA:(h + 1) * KV_LORA] = o[h * seq:(h + 1) * seq, :].astype(o_ref.dtype)


def decode_attention(qa, q, ckv, kr, cache_kv, cache_kr_t, page_table, layer, bs, seq):
    n_pages = page_table.shape[1]
    page = cache_kv.shape[2]
    n = bs * seq
    grid_spec = pltpu.PrefetchScalarGridSpec(
        num_scalar_prefetch=1,
        grid=(bs,),
        in_specs=[
            pl.BlockSpec((seq, MLA_HEADS * KV_LORA), lambda b, pt: (b, 0)),
            pl.BlockSpec((seq, MLA_HEADS * HEAD_PAD), lambda b, pt: (b, 0)),
            pl.BlockSpec((seq, KV_LORA), lambda b, pt: (b, 0)),
            pl.BlockSpec((seq, QK_ROPE), lambda b, pt: (b, 0)),
            pl.BlockSpec(memory_space=pl.ANY),
            pl.BlockSpec(memory_space=pl.ANY),
        ],
        out_specs=pl.BlockSpec((seq, MLA_HEADS * KV_LORA), lambda b, pt: (b, 0)),
        scratch_shapes=[pltpu.VMEM((2, n_pages * page, KV_LORA), F32),
                        pltpu.VMEM((2, QK_ROPE, n_pages * page), F32),
                        pltpu.SemaphoreType.DMA((2,)), pltpu.SemaphoreType.DMA((2,))],
    )
    return pl.pallas_call(
        functools.partial(_decode_attn_kernel, layer=layer, n_pages=n_pages),
        grid_spec=grid_spec,
        out_shape=jax.ShapeDtypeStruct((n, MLA_HEADS * KV_LORA), F32),
        compiler_params=_cparams(("arbitrary",)),
        name="decode_attention",
    )(page_table, qa, q, ckv, kr, cache_kv, cache_kr_t)


def _merge_kernel(x_ref, ys_ref, ym_ref, gs_ref, gm_ref, ws_ref, wm_ref, wo_ref, o_ref):
    ms = _dot(ys_ref[...].astype(BF16), ws_ref[...])
    mm = _dot(ym_ref[...].astype(BF16), wm_ref[...])
    merged = jax.nn.sigmoid(gs_ref[...].astype(F32)) * ms + jax.nn.sigmoid(gm_ref[...].astype(F32)) * mm
    o_ref[...] = x_ref[...] + _dot(merged.astype(BF16), wo_ref[...])


def merge(x, y_ssm, y_mla, proj, w):
    n, d = x.shape
    tm = _pick(n, (512, 256, 128))
    full = lambda a: _resident(a.shape)
    return pl.pallas_call(
        _merge_kernel,
        grid=(n // tm,),
        in_specs=[
            pl.BlockSpec((tm, d), lambda i: (i, 0)),
            pl.BlockSpec((tm, D_INNER), lambda i: (i, 0)),
            pl.BlockSpec((tm, MLA_HEADS * V_HEAD), lambda i: (i, 0)),
            pl.BlockSpec((tm, d), lambda i: (i, C_GS // D_MODEL)),
            pl.BlockSpec((tm, d), lambda i: (i, C_GM // D_MODEL)),
            full(w["w_br_ssm"]), full(w["w_br_mla"]), full(w["w_out"]),
        ],
        out_specs=pl.BlockSpec((tm, d), lambda i: (i, 0)),
        out_shape=jax.ShapeDtypeStruct((n, d), F32),
        compiler_params=_cparams(("parallel",)),
        name="merge",
    )(x, y_ssm, y_mla, proj, proj, w["w_br_ssm"], w["w_br_mla"], w["w_out"])


def _ple_kernel(x_ref, p_ref, g_ref, wg_ref, wp_ref, fg_ref, o_ref, *, final):
    x = x_ref[...]
    gate = jax.nn.sigmoid(_dot(_rms(x, g_ref[...]).astype(BF16), wg_ref[...]))
    x = x + gate * _dot(p_ref[...].astype(BF16), wp_ref[...])
    if final:
        x = _rms(x, fg_ref[...])
    o_ref[...] = x


def ple(x, p, w, final_norm, final):
    n, d = x.shape
    tm = _pick(n, (512, 256, 128))
    full = lambda a: pl.BlockSpec(a.shape, lambda i: (0, 0))
    return pl.pallas_call(
        functools.partial(_ple_kernel, final=final),
        grid=(n // tm,),
        in_specs=[
            pl.BlockSpec((tm, d), lambda i: (i, 0)),
            pl.BlockSpec((tm, PLE_DIM), lambda i: (i, 0)),
            pl.BlockSpec((1, d), lambda i: (0, 0)),
            full(w["w_ple_gate"]), full(w["w_ple_proj"]),
            pl.BlockSpec((1, d), lambda i: (0, 0)),
        ],
        out_specs=pl.BlockSpec((tm, d), lambda i: (i, 0)),
        out_shape=jax.ShapeDtypeStruct((n, d), F32),
        compiler_params=_cparams(("parallel",)),
        name="ple",
    )(x, p, w["ple_norm"], w["w_ple_gate"], w["w_ple_proj"], final_norm)


def _rope_tables(pos, rows):
    half = QK_ROPE // 2
    inv = ROPE_BASE ** (-jnp.arange(half, dtype=F32) / half)
    ang = pos.astype(F32)[:, None] * inv[None, :]
    cos, sin = jnp.cos(ang), jnp.sin(ang)
    z = lambda w_: jnp.zeros((pos.shape[0], w_), F32)
    cos_t = jnp.concatenate([jnp.ones((pos.shape[0], QK_NOPE), F32), cos, cos, z(32)], axis=1)
    sin_hi = jnp.concatenate([z(QK_NOPE + half), sin, z(32)], axis=1)
    sin_lo = jnp.concatenate([z(QK_NOPE), -sin, z(half + 32)], axis=1)
    reps = max(1, rows // pos.shape[0])
    return tuple(jnp.tile(t, (reps, 1)) for t in (cos_t, sin_hi, sin_lo))


def _layer_weights(i, p):
    row = lambda v: v.reshape(1, -1).astype(F32)
    w_in = p["w_in"][i]
    off_z, off_xbc, off_dt, off_q, off_kv = 2048, 5120, 5152, 5664, 5952
    z = w_in[:, :off_z]
    xbc = w_in[:, off_z:off_xbc]
    dt = w_in[:, off_xbc:off_dt]
    q_lat = w_in[:, off_dt:off_q]
    kv_lat = w_in[:, off_q:off_q + KV_LORA]
    k_rope = w_in[:, off_q + KV_LORA:off_kv]
    gates = w_in[:, off_kv:]
    w_main = jnp.concatenate([z, xbc, gates, q_lat], axis=1).astype(BF16)
    w_main_x = jnp.concatenate([w_main, jnp.zeros((D_MODEL, C_DTX - MAIN_DIM), BF16),
                                jnp.repeat(dt, SSM_HEAD_DIM, axis=1).astype(BF16)], axis=1)
    zeros32 = jnp.zeros((D_MODEL, 32), F32)
    small = jnp.concatenate([kv_lat, dt, zeros32, k_rope, zeros32], axis=1)

    lane_pad = lambda v: jnp.pad(v.astype(F32), (0, LANES - v.shape[0])).reshape(1, LANES)
    a_neg = -jnp.exp(p["a_log"][i].astype(F32))

    w_qb = p["w_qb"][i].reshape(Q_LORA, MLA_HEADS, QK_NOPE + QK_ROPE)
    w_qb_pad = jnp.pad(w_qb, ((0, 0), (0, 0), (0, HEAD_PAD - QK_NOPE - QK_ROPE)))
    w_uk = p["w_uk"][i]
    wk_nope = jnp.pad(w_uk, ((0, 0), (0, 0), (0, HEAD_PAD - QK_NOPE))).reshape(KV_LORA, -1)
    place = jnp.zeros((LANES, MLA_HEADS, HEAD_PAD), F32)
    j = jnp.arange(QK_ROPE)
    place = place.at[S_ROPE - KV_LORA + j, :, QK_NOPE + j].set(1.0).reshape(LANES, -1)
    w_abs = jnp.pad(jnp.transpose(w_uk, (1, 2, 0)), ((0, 0), (0, HEAD_PAD - QK_NOPE), (0, 0)))
    w_uv = p["w_uv"][i]
    eye = jnp.eye(MLA_HEADS, dtype=F32)
    w_uv_bd = (jnp.transpose(w_uv, (1, 0, 2))[:, :, None, :] * eye[:, None, :, None]).reshape(
        MLA_HEADS * KV_LORA, MLA_HEADS * V_HEAD)

    return {
        "ffn1_norm": row(p["ffn1_norm"][i]), "ffn1_w_gu": p["ffn1_w_gu"][i].astype(BF16),
        "ffn1_w_down": p["ffn1_w_down"][i].astype(BF16),
        "ffn2_norm": row(p["ffn2_norm"][i]), "ffn2_w_gu": p["ffn2_w_gu"][i].astype(BF16),
        "ffn2_w_down": p["ffn2_w_down"][i].astype(BF16),
        "mix_norm": row(p["mix_norm"][i]),
        "w_main": w_main, "w_main_x": w_main_x,
        "w_small": small.astype(BF16),
        "conv_w": p["conv_w"][i].astype(F32), "conv_b": row(p["conv_b"][i]),
        "dt_bias_l": lane_pad(p["dt_bias"][i]), "a_neg_l": lane_pad(a_neg),
        "dt_bias_x": row(jnp.repeat(p["dt_bias"][i], SSM_HEAD_DIM)),
        "a_neg_x": row(jnp.repeat(a_neg, SSM_HEAD_DIM)),
        "d_skip_x": row(jnp.repeat(p["d_skip"][i], SSM_HEAD_DIM)),
        "ssm_norm": row(p["ssm_norm"][i]),
        "q_norm": row(p["q_norm"][i]), "kv_norm": row(p["kv_norm"][i]),
        "w_qb_pad": w_qb_pad.reshape(Q_LORA, -1).astype(BF16),
        "wk_full": jnp.concatenate([wk_nope, place], axis=0).astype(BF16),
        "w_uv_t": w_uv.reshape(KV_LORA, -1).T.astype(BF16),
        "w_abs": w_abs.astype(BF16),
        "w_uv_bd": w_uv_bd.astype(BF16),
        "w_br_ssm": p["w_br_ssm"][i].astype(BF16), "w_br_mla": p["w_br_mla"][i].astype(BF16),
        "w_out": p["w_out"][i].astype(BF16),
        "ple_norm": row(p["ple_norm"][i]), "w_ple_gate": p["w_ple_gate"][i].astype(BF16),
        "w_ple_proj": p["w_ple_proj"][i].astype(BF16),
    }


def _trunk_layer(x, p_l, tabs, conv_prev, ssm_prev, w, bs, seq, final_norm, final, paged):
    x = ffn_half_step(x, w["ffn1_norm"], w["ffn1_w_gu"], w["ffn1_w_down"])
    proj = norm_matmul(x, w["mix_norm"], w["w_main"] if paged is None else w["w_main_x"], F32)
    small = norm_matmul(x, w["mix_norm"], w["w_small"], F32)
    if paged is None:
        y_ssm, new_conv, new_ssm = ssm_prefill(proj, small, conv_prev, ssm_prev, w, bs, seq)
        c_kv, k_rope, kfull, vt = kv_project(small, w, tabs, True)
        q = q_project(proj, w, tabs, False)
        y_mla = prefill_attention(q, kfull, vt, bs, seq)
    else:
        cache_kv, cache_kr_t, page_table, layer, new_ssm_stack = paged
        y_ssm, new_conv, new_ssm = ssm_step(proj, conv_prev, ssm_prev, new_ssm_stack, layer, w, bs, seq)
        c_kv, k_rope = kv_project(small, w, tabs, False)
        q, qa = q_project(proj, w, tabs, True)
        o_lat = decode_attention(qa, q, c_kv, k_rope, cache_kv, cache_kr_t, page_table, layer, bs, seq)
        y_mla = norm_matmul(o_lat, None, w["w_uv_bd"], BF16, norm=False, tm_prefs=(256, 128))
    x = merge(x, y_ssm, y_mla, proj, w)
    x = ffn_half_step(x, w["ffn2_norm"], w["ffn2_w_gu"], w["ffn2_w_down"])
    x = ple(x, p_l, w, final_norm, final)
    return x, c_kv, k_rope, new_conv, new_ssm


def kernel(x_prompt, x_sample, cache_kv, cache_kr, state_conv, state_ssm, page_table, p_prompt, p_sample,
           ffn1_norm, ffn1_w_gu, ffn1_w_down, mix_norm, w_in, conv_w, conv_b, dt_bias, a_log, d_skip,
           ssm_norm, q_norm, w_qb, kv_norm, w_uk, w_uv, w_br_ssm, w_br_mla, w_out,
           ffn2_norm, ffn2_w_gu, ffn2_w_down, ple_norm, w_ple_gate, w_ple_proj, final_norm):
    params = dict(ffn1_norm=ffn1_norm, ffn1_w_gu=ffn1_w_gu, ffn1_w_down=ffn1_w_down, mix_norm=mix_norm,
                  w_in=w_in, conv_w=conv_w, conv_b=conv_b, dt_bias=dt_bias, a_log=a_log, d_skip=d_skip,
                  ssm_norm=ssm_norm, q_norm=q_norm, w_qb=w_qb, kv_norm=kv_norm, w_uk=w_uk, w_uv=w_uv,
                  w_br_ssm=w_br_ssm, w_br_mla=w_br_mla, w_out=w_out, ffn2_norm=ffn2_norm,
                  ffn2_w_gu=ffn2_w_gu, ffn2_w_down=ffn2_w_down, ple_norm=ple_norm,
                  w_ple_gate=w_ple_gate, w_ple_proj=w_ple_proj)
    depth = w_in.shape[0]
    bp, sp, d = x_prompt.shape
    bsm, ss, _ = x_sample.shape
    past_len = page_table.shape[1] * cache_kv.shape[2]
    np_, ns = bp * sp, bsm * ss

    tabs_p = _rope_tables(jnp.arange(sp), sp)
    tabs_s = _rope_tables(past_len + jnp.arange(ss), _pick(ns, (256, 128)))
    conv0 = jnp.zeros((bp, D_CONV - 1, CONV_DIM), F32)
    ssm0 = jnp.zeros((bp, SSM_HEADS, SSM_HEAD_DIM, SSM_STATE), F32)
    fnorm = final_norm.reshape(1, -1).astype(F32)

    cache_kr_t = jnp.swapaxes(cache_kr, 2, 3)

    hp = x_prompt.reshape(np_, d)
    hs = x_sample.reshape(ns, d)
    outs_p, outs_s = [], []
    new_ssm_stack = None
    for i in range(depth):
        last = i == depth - 1
        w = _layer_weights(i, params)
        hs, *rest_s = _trunk_layer(hs, p_sample[i].reshape(ns, PLE_DIM), tabs_s, state_conv,
                                   state_ssm, w, bsm, ss, fnorm, last,
                                   (cache_kv, cache_kr_t, page_table, i, new_ssm_stack))
        hp, *rest_p = _trunk_layer(hp, p_prompt[i].reshape(np_, PLE_DIM), tabs_p, conv0, ssm0, w,
                                   bp, sp, fnorm, last, None)
        new_ssm_stack = rest_s[3]
        outs_p.append(rest_p)
        outs_s.append(rest_s)

    stack = lambda outs, k, shape: jnp.stack([o[k] for o in outs]).reshape((depth,) + shape)
    return (
        hp.reshape(bp, sp, d), hs.reshape(bsm, ss, d),
        stack(outs_p, 0, (bp, sp, KV_LORA)), stack(outs_p, 1, (bp, sp, QK_ROPE)),
        stack(outs_p, 2, (bp, D_CONV - 1, CONV_DIM)),
        stack(outs_p, 3, (bp, SSM_HEADS, SSM_HEAD_DIM, SSM_STATE)),
        stack(outs_s, 0, (bsm, ss, KV_LORA)), stack(outs_s, 1, (bsm, ss, QK_ROPE)),
        stack(outs_s, 2, (bsm, D_CONV - 1, CONV_DIM)),
        new_ssm_stack,
    )
```

```python
import functools
import math

import jax
import jax.numpy as jnp
from jax import lax
from jax.experimental import pallas as pl
from jax.experimental.pallas import tpu as pltpu

F32 = jnp.float32
BF16 = jnp.bfloat16

D_MODEL = 1024
D_FF = 2816
D_INNER = 2048
SSM_HEAD_DIM = 64
SSM_HEADS = 32
SSM_GROUPS = 4
HEADS_PER_GROUP = SSM_HEADS // SSM_GROUPS
SSM_STATE = 128
GROUP_CH = D_INNER // SSM_GROUPS
D_CONV = 4
CONV_DIM = D_INNER + 2 * SSM_GROUPS * SSM_STATE
SSD_CHUNK = 128
MLA_HEADS = 16
Q_LORA = 512
KV_LORA = 256
QK_NOPE = 64
QK_ROPE = 32
V_HEAD = 64
ROPE_BASE = 10000.0
ATTN_SCALE = (QK_NOPE + QK_ROPE) ** -0.5
SCORE_SCALE = ATTN_SCALE * math.log2(math.e)
PLE_DIM = 256
EPS = 1e-6
LANES = 128
HEAD_PAD = 128
DECODE_CHUNK_PAGES = 4
NEG_BIG = -1e30

C_Z, C_XS, C_BC, C_GS, C_GM, C_Q = 0, 2048, 4096, 5120, 6144, 7168
MAIN_DIM = 7680
C_DTX = 8192
MAIN_DIM_X = C_DTX + D_INNER
SMALL_DIM = 384
S_DT, S_ROPE = 256, 320

VMEM_LIMIT = 56 * 1024 * 1024


def _cparams(sem):
    return pltpu.CompilerParams(dimension_semantics=sem, vmem_limit_bytes=VMEM_LIMIT)


def _pick(n, prefs):
    for p in prefs:
        if n % p == 0:
            return p
    return n


def _dot(a, b):
    return jnp.dot(a, b, preferred_element_type=F32)


def _dot_nt(a, b):
    return lax.dot_general(a, b, (((1,), (1,)), ((), ())), preferred_element_type=F32)


def _silu(x):
    return x * jax.nn.sigmoid(x)


def _softplus(x):
    return jnp.maximum(x, 0.0) + jnp.log1p(jnp.exp(-jnp.abs(x)))


def _rms(x, g):
    return x * lax.rsqrt(jnp.mean(x * x, axis=-1, keepdims=True) + EPS) * g


def _nm_kernel(x_ref, g_ref, w_ref, o_ref, u_sc, *, norm):
    @pl.when(pl.program_id(1) == 0)
    def _():
        x = x_ref[...].astype(F32)
        if norm:
            x = _rms(x, g_ref[...])
        u_sc[...] = x.astype(BF16)

    o_ref[...] = _dot(u_sc[...], w_ref[...]).astype(o_ref.dtype)


def norm_matmul(x, g, w, out_dtype, *, norm=True, col_block=0, tm_prefs=(1024, 512, 256, 128)):
    n = x.shape[0]
    k, nout = w.shape
    tm = _pick(n, tm_prefs)
    tn = _pick(nout, (1280, 1024, 512, 384, 256, 128))
    if not norm:
        g = jnp.ones((1, k), F32)
    return pl.pallas_call(
        functools.partial(_nm_kernel, norm=norm),
        grid=(n // tm, nout // tn),
        in_specs=[
            pl.BlockSpec((tm, k), lambda i, j: (i, col_block)),
            pl.BlockSpec((1, k), lambda i, j: (0, 0)),
            pl.BlockSpec((k, tn), lambda i, j: (0, j)),
        ],
        out_specs=pl.BlockSpec((tm, tn), lambda i, j: (i, j)),
        out_shape=jax.ShapeDtypeStruct((n, nout), out_dtype),
        scratch_shapes=[pltpu.VMEM((tm, k), BF16)],
        compiler_params=_cparams(("parallel", "arbitrary")),
        name="norm_matmul",
    )(x, g, w)


def _resident(shape):
    return pl.BlockSpec(shape, lambda *_: (0,) * len(shape), pipeline_mode=pl.Buffered(1))


def _resident_layer(stacked_shape, layer):
    return pl.BlockSpec((None,) + tuple(stacked_shape[1:]),
                        lambda *_: (layer,) + (0,) * (len(stacked_shape) - 1),
                        pipeline_mode=pl.Buffered(1))


def _cast_kernel(x_ref, o_ref):
    o_ref[...] = x_ref[...].astype(o_ref.dtype)


def cast_bf16(x):
    depth, rows, cols = x.shape
    tr = _pick(rows, (256, 128))
    return pl.pallas_call(
        _cast_kernel,
        grid=(depth, rows // tr),
        in_specs=[pl.BlockSpec((1, tr, cols), lambda l, i: (l, i, 0))],
        out_specs=pl.BlockSpec((1, tr, cols), lambda l, i: (l, i, 0)),
        out_shape=jax.ShapeDtypeStruct(x.shape, BF16),
        compiler_params=_cparams(("parallel", "parallel")),
        name="cast_bf16",
    )(x)


def _in_proj_kernel(x_ref, g_ref, wm_ref, ws_ref, om_ref, os_ref, *, chunk):
    u = _rms(x_ref[...], g_ref[...]).astype(BF16)
    for c0 in range(0, wm_ref.shape[1], chunk):
        om_ref[:, c0:c0 + chunk] = _dot(u, wm_ref[:, c0:c0 + chunk]).astype(om_ref.dtype)
    os_ref[...] = _dot(u, ws_ref[...])


def in_project(x, g, w_main, w_small, main_dtype, tm_prefs):
    n, d = x.shape
    nmain = w_main.shape[1]
    tm = _pick(n, tm_prefs)
    return pl.pallas_call(
        functools.partial(_in_proj_kernel, chunk=_pick(nmain, (1280, 1024, 512))),
        grid=(n // tm,),
        in_specs=[
            pl.BlockSpec((tm, d), lambda i: (i, 0)),
            pl.BlockSpec((1, d), lambda i: (0, 0)),
            _resident(w_main.shape),
            _resident(w_small.shape),
        ],
        out_specs=[pl.BlockSpec((tm, nmain), lambda i: (i, 0)),
                   pl.BlockSpec((tm, SMALL_DIM), lambda i: (i, 0))],
        out_shape=[jax.ShapeDtypeStruct((n, nmain), main_dtype),
                   jax.ShapeDtypeStruct((n, SMALL_DIM), F32)],
        compiler_params=_cparams(("parallel",)),
        name="in_project",
    )(x, g, w_main, w_small)


def _ffn_kernel(x_ref, g_ref, wgu_ref, wd_ref, o_ref):
    x = x_ref[...]
    dff = wd_ref.shape[0]
    u = _rms(x, g_ref[...]).astype(BF16)
    a = _dot(u, wgu_ref[:, :dff])
    b = _dot(u, wgu_ref[:, dff:])
    h = (_silu(a) * b).astype(BF16)
    o_ref[...] = x + 0.5 * _dot(h, wd_ref[...])


def ffn_half_step(x, g, w_gu, w_down, layer):
    n, d = x.shape
    tm = _pick(n, (512, 256, 128))
    return pl.pallas_call(
        _ffn_kernel,
        grid=(n // tm,),
        in_specs=[
            pl.BlockSpec((tm, d), lambda i: (i, 0)),
            pl.BlockSpec((1, d), lambda i: (0, 0)),
            _resident_layer(w_gu.shape, layer),
            _resident_layer(w_down.shape, layer),
        ],
        out_specs=pl.BlockSpec((tm, d), lambda i: (i, 0)),
        out_shape=jax.ShapeDtypeStruct((n, d), F32),
        compiler_params=_cparams(("parallel",)),
        name="ffn_half_step",
    )(x, g, w_gu, w_down)


def _causal_conv(xpad_sc, rows, w_ref, b_ref):
    w = w_ref[...]
    acc = xpad_sc[5:5 + rows, :] * w[0:1, :]
    for k in range(1, D_CONV):
        acc = acc + xpad_sc[5 + k:5 + k + rows, :] * w[k:k + 1, :]
    return _silu(acc + b_ref[...])


def _gate_and_group_norm(y, z, norm_w):
    y = y * _silu(z)
    parts = []
    for g in range(SSM_GROUPS):
        yg = y[:, g * GROUP_CH:(g + 1) * GROUP_CH]
        parts.append(yg * lax.rsqrt(jnp.mean(yg * yg, axis=-1, keepdims=True) + EPS))
    return jnp.concatenate(parts, axis=1) * norm_w


def _cumsum_rows(a):
    rows = a.shape[0]
    row = lax.broadcasted_iota(jnp.int32, a.shape, 0)
    s = 1
    while s < rows:
        a = a + jnp.where(row >= s, pltpu.roll(a, s, axis=0), 0.0)
        s *= 2
    return a


def _ssm_prefill_kernel(z_ref, xs_ref, bc_ref, dt_ref, cprev_ref, sprev_ref, convw_ref, convb_ref,
                        dtb_ref, aneg_ref, dskip_ref, norm_ref,
                        y_ref, nconv_ref, nssm_ref, xpad_sc, h_sc):
    c = pl.program_id(1)
    t = SSD_CHUNK
    hp = SSM_HEAD_DIM

    @pl.when(c == 0)
    def _():
        xpad_sc[5:8, :] = cprev_ref[0]
        h_sc[...] = sprev_ref[0].reshape(D_INNER, SSM_STATE)

    xpad_sc[8:8 + t, 0:D_INNER] = xs_ref[...].astype(F32)
    xpad_sc[8:8 + t, D_INNER:CONV_DIM] = bc_ref[...].astype(F32)
    conv = _causal_conv(xpad_sc, t, convw_ref, convb_ref)
    tail = xpad_sc[8 + t - 3:8 + t, :]
    xpad_sc[5:8, :] = tail

    xs = conv[:, :D_INNER]
    gs = SSM_GROUPS * SSM_STATE
    bm = conv[:, D_INNER:D_INNER + gs]
    cm = conv[:, D_INNER + gs:]

    dt = _softplus(dt_ref[...] + dtb_ref[...])
    acum = _cumsum_rows(dt * aneg_ref[...])
    alast = acum[t - 1:t, :]
    acum_t = acum.T
    dt_t = dt.T
    wst_t = (dt * jnp.exp(alast - acum)).T
    xs_t = xs.T

    tri = (lax.broadcasted_iota(jnp.int32, (t, t), 0) >= lax.broadcasted_iota(jnp.int32, (t, t), 1))
    low_half = lax.broadcasted_iota(jnp.int32, (t, 2 * hp), 1) < hp

    y_parts = []
    for g in range(SSM_GROUPS):
        bg = bm[:, g * SSM_STATE:(g + 1) * SSM_STATE]
        cg = cm[:, g * SSM_STATE:(g + 1) * SSM_STATE]
        cb = _dot_nt(cg.astype(BF16), bg.astype(BF16))
        for pair in range(HEADS_PER_GROUP // 2):
            h0 = g * HEADS_PER_GROUP + 2 * pair
            lhs = []
            for h in (h0, h0 + 1):
                colb = jnp.broadcast_to(acum[:, h:h + 1], (t, t))
                decay = jnp.where(tri, jnp.exp(colb - acum_t[h:h + 1, :]), 0.0)
                m = cb * decay * dt_t[h:h + 1, :]
                ce = cg * jnp.exp(colb)
                lhs.append(jnp.concatenate([m, ce], axis=1))
            lhs = jnp.concatenate(lhs, axis=0).astype(BF16)
            r0 = h0 * hp
            rhs_t = jnp.concatenate([xs_t[r0:r0 + 2 * hp, :], h_sc[r0:r0 + 2 * hp, :]],
                                    axis=1).astype(BF16)
            out = _dot_nt(lhs, rhs_t)
            y_parts.append(jnp.where(low_half, out[:t, :], out[t:, :]))
    y = jnp.concatenate(y_parts, axis=1) + dskip_ref[...] * xs
    y_ref[...] = _gate_and_group_norm(y, z_ref[...].astype(F32), norm_ref[...]).astype(y_ref.dtype)

    elast = jnp.exp(alast)
    for g in range(SSM_GROUPS):
        bg = bm[:, g * SSM_STATE:(g + 1) * SSM_STATE].astype(BF16)
        xw = []
        for hh in range(HEADS_PER_GROUP):
            h = g * HEADS_PER_GROUP + hh
            xw.append(xs_t[h * hp:(h + 1) * hp, :] * wst_t[h:h + 1, :])
        dh = _dot(jnp.concatenate(xw, axis=0).astype(BF16), bg)
        for hh in range(HEADS_PER_GROUP):
            h = g * HEADS_PER_GROUP + hh
            rows = slice(h * hp, (h + 1) * hp)
            h_sc[rows, :] = h_sc[rows, :] * elast[:, h:h + 1] + dh[hh * hp:(hh + 1) * hp, :]

    @pl.when(c == pl.num_programs(1) - 1)
    def _():
        nconv_ref[0] = tail
        nssm_ref[0] = h_sc[...].reshape(SSM_HEADS, SSM_HEAD_DIM, SSM_STATE)


def ssm_prefill(proj, small, conv_prev, ssm_prev, w, bs, seq):
    t = SSD_CHUNK
    nc = seq // t
    n = bs * seq
    row = lambda b, c: b * nc + c
    vec = lambda width: pl.BlockSpec((1, width), lambda b, c: (0, 0))
    return pl.pallas_call(
        _ssm_prefill_kernel,
        grid=(bs, nc),
        in_specs=[
            pl.BlockSpec((t, D_INNER), lambda b, c: (row(b, c), C_Z // D_INNER)),
            pl.BlockSpec((t, D_INNER), lambda b, c: (row(b, c), C_XS // D_INNER)),
            pl.BlockSpec((t, 1024), lambda b, c: (row(b, c), C_BC // 1024)),
            pl.BlockSpec((t, LANES), lambda b, c: (row(b, c), S_DT // LANES)),
            pl.BlockSpec((1, D_CONV - 1, CONV_DIM), lambda b, c: (b, 0, 0)),
            pl.BlockSpec((1, SSM_HEADS, SSM_HEAD_DIM, SSM_STATE), lambda b, c: (b, 0, 0, 0)),
            pl.BlockSpec((D_CONV, CONV_DIM), lambda b, c: (0, 0)),
            vec(CONV_DIM), vec(LANES), vec(LANES), vec(D_INNER), vec(D_INNER),
        ],
        out_specs=[
            pl.BlockSpec((t, D_INNER), lambda b, c: (row(b, c), 0)),
            pl.BlockSpec((1, D_CONV - 1, CONV_DIM), lambda b, c: (b, 0, 0)),
            pl.BlockSpec((1, SSM_HEADS, SSM_HEAD_DIM, SSM_STATE), lambda b, c: (b, 0, 0, 0)),
        ],
        out_shape=[
            jax.ShapeDtypeStruct((n, D_INNER), F32),
            jax.ShapeDtypeStruct((bs, D_CONV - 1, CONV_DIM), F32),
            jax.ShapeDtypeStruct((bs, SSM_HEADS, SSM_HEAD_DIM, SSM_STATE), F32),
        ],
        scratch_shapes=[pltpu.VMEM((t + 8, CONV_DIM), F32), pltpu.VMEM((D_INNER, SSM_STATE), F32)],
        compiler_params=_cparams(("parallel", "arbitrary")),
        name="ssm_prefill",
    )(proj, proj, proj, small, conv_prev, ssm_prev, w["conv_w"], w["conv_b"], w["dt_bias_l"],
      w["a_neg_l"], w["d_skip_x"], w["ssm_norm"])


def _ssm_step_kernel(z_ref, xs_ref, bc_ref, dtx_ref, cprev_ref, sprev_ref, convw_ref, convb_ref,
                     dtbx_ref, anegx_ref, dskip_ref, norm_ref, *rest):
    y_ref, nconv_ref, nssm_ref, xpad_sc, xw_sc, bpad_sc = rest[-6:]
    seq = xs_ref.shape[0]
    hp = SSM_HEAD_DIM
    gs = SSM_GROUPS * SSM_STATE

    @pl.when(pl.program_id(0) == 0)
    def _():
        xw_sc[...] = jnp.zeros_like(xw_sc)
        bpad_sc[...] = jnp.zeros_like(bpad_sc)

    xpad_sc[5:8, :] = cprev_ref[0, 0]
    xpad_sc[8:8 + seq, 0:D_INNER] = xs_ref[...]
    xpad_sc[8:8 + seq, D_INNER:CONV_DIM] = bc_ref[...]
    conv = _causal_conv(xpad_sc, seq, convw_ref, convb_ref)
    nconv_ref[0] = xpad_sc[8 + seq - 3:8 + seq, :]

    xs = conv[:, :D_INNER]
    bm = conv[:, D_INNER:D_INNER + gs]
    cm = conv[:, D_INNER + gs:]

    dtx = _softplus(dtx_ref[...] + dtbx_ref[...])
    acum = _cumsum_rows(dtx * anegx_ref[...])
    alast = acum[seq - 1:seq, :]
    xd = xs * dtx

    c_all = jnp.concatenate([cm[:, g * SSM_STATE:(g + 1) * SSM_STATE] for g in range(SSM_GROUPS)],
                            axis=0).astype(BF16)
    y_off = []
    for g in range(SSM_GROUPS):
        hg = sprev_ref[0, 0, g * HEADS_PER_GROUP:(g + 1) * HEADS_PER_GROUP].reshape(GROUP_CH, SSM_STATE)
        y_off.append(_dot_nt(c_all, hg.astype(BF16))[g * seq:(g + 1) * seq, :])
    y = jnp.concatenate(y_off, axis=1) * jnp.exp(acum)

    ones = jnp.ones((SSM_STATE, GROUP_CH), BF16)
    cbx = []
    for g in range(SSM_GROUPS):
        bg = bm[:, g * SSM_STATE:(g + 1) * SSM_STATE]
        cg = cm[:, g * SSM_STATE:(g + 1) * SSM_STATE]
        prod = jnp.concatenate([cg * bg[s:s + 1, :] for s in range(seq)], axis=0)
        hi = prod.astype(BF16)
        lo = (prod - hi.astype(F32)).astype(BF16)
        cbx.append(_dot(hi, ones) + _dot(lo, ones))
    row = lax.broadcasted_iota(jnp.int32, (seq, D_INNER), 0)
    for s in range(seq):
        cb_s = jnp.concatenate([cbx[g][s * seq:(s + 1) * seq, :] for g in range(SSM_GROUPS)], axis=1)
        decay = jnp.where(row >= s, jnp.exp(acum - acum[s:s + 1, :]), 0.0)
        y = y + cb_s * decay * xd[s:s + 1, :]

    y = y + dskip_ref[...] * xs
    y_ref[...] = _gate_and_group_norm(y, z_ref[...], norm_ref[...]).astype(y_ref.dtype)

    xw_sc[0:seq, :] = xd * jnp.exp(alast - acum)
    bpad_sc[0:seq, :] = bm
    xw_t = xw_sc[...].T
    elast = jnp.exp(alast)
    for g in range(SSM_GROUPS):
        dh = _dot(xw_t[g * GROUP_CH:(g + 1) * GROUP_CH, :].astype(BF16),
                  bpad_sc[:, g * SSM_STATE:(g + 1) * SSM_STATE].astype(BF16))
        for hh in range(HEADS_PER_GROUP):
            h = g * HEADS_PER_GROUP + hh
            nssm_ref[0, 0, h] = (sprev_ref[0, 0, h] * elast[:, h * hp:h * hp + 1]
                                 + dh[hh * hp:(hh + 1) * hp, :])


def ssm_step(proj, state_conv, state_ssm, new_ssm_stack, layer, w, bs, seq):
    n = bs * seq
    vec = lambda width: pl.BlockSpec((1, width), lambda b: (0, 0))
    state_block = (1, 1, SSM_HEADS, SSM_HEAD_DIM, SSM_STATE)
    in_specs = [
        pl.BlockSpec((seq, D_INNER), lambda b: (b, C_Z // D_INNER)),
        pl.BlockSpec((seq, D_INNER), lambda b: (b, C_XS // D_INNER)),
        pl.BlockSpec((seq, 1024), lambda b: (b, C_BC // 1024)),
        pl.BlockSpec((seq, D_INNER), lambda b: (b, C_DTX // D_INNER)),
        pl.BlockSpec((1, 1, D_CONV - 1, CONV_DIM), lambda b: (layer, b, 0, 0)),
        pl.BlockSpec(state_block, lambda b: (layer, b, 0, 0, 0)),
        pl.BlockSpec((D_CONV, CONV_DIM), lambda b: (0, 0)),
        vec(CONV_DIM), vec(D_INNER), vec(D_INNER), vec(D_INNER), vec(D_INNER),
    ]
    args = [proj, proj, proj, proj, state_conv, state_ssm, w["conv_w"], w["conv_b"], w["dt_bias_x"],
            w["a_neg_x"], w["d_skip_x"], w["ssm_norm"]]
    aliases = {}
    if new_ssm_stack is not None:
        in_specs.append(pl.BlockSpec(memory_space=pl.ANY))
        args.append(new_ssm_stack)
        aliases = {len(args) - 1: 2}
    return pl.pallas_call(
        _ssm_step_kernel,
        grid=(bs,),
        in_specs=in_specs,
        out_specs=[
            pl.BlockSpec((seq, D_INNER), lambda b: (b, 0)),
            pl.BlockSpec((1, D_CONV - 1, CONV_DIM), lambda b: (b, 0, 0)),
            pl.BlockSpec(state_block, lambda b: (layer, b, 0, 0, 0)),
        ],
        out_shape=[
            jax.ShapeDtypeStruct((n, D_INNER), F32),
            jax.ShapeDtypeStruct((bs, D_CONV - 1, CONV_DIM), F32),
            jax.ShapeDtypeStruct(state_ssm.shape, F32),
        ],
        scratch_shapes=[pltpu.VMEM((seq + 8, CONV_DIM), F32), pltpu.VMEM((LANES, D_INNER), F32),
                        pltpu.VMEM((LANES, SSM_GROUPS * SSM_STATE), F32)],
        input_output_aliases=aliases,
        compiler_params=_cparams(("arbitrary",)),
        name="ssm_step",
    )(*args)


def _rope(x, cos, sin_hi, sin_lo):
    return x * cos + pltpu.roll(x, 16, axis=1) * sin_hi + pltpu.roll(x, x.shape[1] - 16, axis=1) * sin_lo


def _kv_kernel(s_ref, g_ref, cos_ref, shi_ref, slo_ref, wk_ref, wv_ref,
               ckv_ref, kr_ref, *kv_out, with_kv):
    blk = s_ref[...]
    ckv = _rms(blk[:, :KV_LORA], g_ref[...])
    ckv_ref[...] = ckv
    tail = _rope(blk[:, KV_LORA:], cos_ref[...], shi_ref[...], slo_ref[...])
    kr_ref[...] = tail[:, S_ROPE - KV_LORA:S_ROPE - KV_LORA + QK_ROPE]
    if with_kv:
        kfull_ref, vt_ref = kv_out
        ckv16 = ckv.astype(BF16)
        lhs = jnp.concatenate([ckv16, tail.astype(BF16)], axis=1)
        kfull_ref[...] = _dot(lhs, wk_ref[...]).astype(kfull_ref.dtype)
        vt_ref[0] = _dot_nt(wv_ref[...], ckv16).astype(vt_ref.dtype)


def kv_project(small, w, tabs, with_kv):
    n = small.shape[0]
    tm = min(_pick(n, (512, 256, 128)), tabs[0].shape[0])
    ntab = tabs[0].shape[0] // tm
    tab = pl.BlockSpec((tm, LANES), lambda i: (i % ntab, 0))
    out_specs = [pl.BlockSpec((tm, KV_LORA), lambda i: (i, 0)), pl.BlockSpec((tm, QK_ROPE), lambda i: (i, 0))]
    out_shape = [jax.ShapeDtypeStruct((n, KV_LORA), F32), jax.ShapeDtypeStruct((n, QK_ROPE), F32)]
    if with_kv:
        out_specs += [pl.BlockSpec((tm, MLA_HEADS * HEAD_PAD), lambda i: (i, 0)),
                      pl.BlockSpec((1, MLA_HEADS * V_HEAD, tm), lambda i: (i, 0, 0))]
        out_shape += [jax.ShapeDtypeStruct((n, MLA_HEADS * HEAD_PAD), BF16),
                      jax.ShapeDtypeStruct((n // tm, MLA_HEADS * V_HEAD, tm), BF16)]
    return pl.pallas_call(
        functools.partial(_kv_kernel, with_kv=with_kv),
        grid=(n // tm,),
        in_specs=[
            pl.BlockSpec((tm, SMALL_DIM), lambda i: (i, 0)),
            pl.BlockSpec((1, KV_LORA), lambda i: (0, 0)),
            tab, tab, tab,
            pl.BlockSpec(w["wk_full"].shape, lambda i: (0, 0)),
            pl.BlockSpec(w["w_uv_t"].shape, lambda i: (0, 0)),
        ],
        out_specs=out_specs,
        out_shape=out_shape,
        compiler_params=_cparams(("parallel",)),
        name="kv_project",
    )(small, w["kv_norm"], *tabs, w["wk_full"], w["w_uv_t"])


def _q_kernel(x_ref, g_ref, cos_ref, shi_ref, slo_ref, wq_ref, *rest, absorb):
    u = _rms(x_ref[...].astype(F32), g_ref[...]).astype(BF16)
    q = _dot(u, wq_ref[...])
    tile = lambda r: jnp.concatenate([r[...]] * MLA_HEADS, axis=1)
    q = _rope(q, tile(cos_ref), tile(shi_ref), tile(slo_ref))
    if absorb:
        wabs_ref, q_ref, qa_ref = rest
        q_ref[...] = q.astype(q_ref.dtype)
        q16 = q.astype(BF16)
        for h in range(MLA_HEADS):
            qa_ref[:, h * KV_LORA:(h + 1) * KV_LORA] = _dot(
                q16[:, h * HEAD_PAD:(h + 1) * HEAD_PAD], wabs_ref[h]).astype(qa_ref.dtype)
    else:
        (q_ref,) = rest
        q_ref[...] = (q * SCORE_SCALE).astype(q_ref.dtype)


def q_project(proj, w, tabs, absorb):
    n = proj.shape[0]
    tm = min(_pick(n, (512, 256, 128)), tabs[0].shape[0])
    ntab = tabs[0].shape[0] // tm
    tab = pl.BlockSpec((tm, LANES), lambda i: (i % ntab, 0))
    qw = MLA_HEADS * HEAD_PAD
    in_specs = [
        pl.BlockSpec((tm, Q_LORA), lambda i: (i, C_Q // Q_LORA)),
        pl.BlockSpec((1, Q_LORA), lambda i: (0, 0)),
        tab, tab, tab,
        pl.BlockSpec((Q_LORA, qw), lambda i: (0, 0)),
    ]
    args = [proj, w["q_norm"], *tabs, w["w_qb_pad"]]
    if absorb:
        in_specs.append(pl.BlockSpec((MLA_HEADS, HEAD_PAD, KV_LORA), lambda i: (0, 0, 0)))
        args.append(w["w_abs"])
        out_specs = [pl.BlockSpec((tm, qw), lambda i: (i, 0)),
                     pl.BlockSpec((tm, MLA_HEADS * KV_LORA), lambda i: (i, 0))]
        out_shape = [jax.ShapeDtypeStruct((n, qw), F32),
                     jax.ShapeDtypeStruct((n, MLA_HEADS * KV_LORA), F32)]
    else:
        out_specs = pl.BlockSpec((tm, qw), lambda i: (i, 0))
        out_shape = jax.ShapeDtypeStruct((n, qw), BF16)
    return pl.pallas_call(
        functools.partial(_q_kernel, absorb=absorb),
        grid=(n // tm,),
        in_specs=in_specs,
        out_specs=out_specs,
        out_shape=out_shape,
        compiler_params=_cparams(("parallel",)),
        name="q_project",
    )(*args)


def _bcast_lanes(x, width):
    return jnp.concatenate([x] * (width // LANES), axis=1)


def _prefill_attn_kernel(q_ref, k_ref, vt_ref, o_ref, m_sc, l_sc, acc_sc):
    qi = pl.program_id(2)
    tq = q_ref.shape[0]
    tk = vt_ref.shape[2]
    nsplit, _, tc = acc_sc.shape
    head0 = lax.broadcasted_iota(jnp.int32, (2 * V_HEAD, tc), 0) < V_HEAD

    m_sc[...] = jnp.full_like(m_sc, NEG_BIG)
    l_sc[...] = jnp.zeros_like(l_sc)
    acc_sc[...] = jnp.zeros_like(acc_sc)

    def update(kis, masked):
        scores = []
        for ki in kis:
            start = pl.multiple_of(ki * tk, tk)
            for c in range(nsplit):
                for h in range(2):
                    qh = q_ref[c * tc:(c + 1) * tc, h * HEAD_PAD:(h + 1) * HEAD_PAD]
                    kh = k_ref[pl.ds(start, tk), h * HEAD_PAD:(h + 1) * HEAD_PAD]
                    scores.append(_dot_nt(kh, qh))
        for n, ki in enumerate(kis):
            vt = vt_ref[ki]
            for c in range(nsplit):
                alphas, pvs = [], []
                for h in range(2):
                    st = c * 2 + h
                    s = scores[n * 2 * nsplit + st]
                    if masked:
                        keep = (lax.broadcasted_iota(jnp.int32, (tk, tc), 0)
                                <= lax.broadcasted_iota(jnp.int32, (tk, tc), 1) + c * tc)
                        s = jnp.where(keep, s, NEG_BIG)
                    m_prev = m_sc[st]
                    m_next = jnp.maximum(m_prev, jnp.max(s, axis=0, keepdims=True))
                    p = jnp.exp2(s - m_next)
                    alpha = jnp.exp2(m_prev - m_next)
                    l_sc[st] = alpha * l_sc[st] + jnp.sum(p, axis=0, keepdims=True)
                    m_sc[st] = m_next
                    alphas.append(alpha)
                    pvs.append(_dot(vt, p.astype(BF16)))
                acc_sc[c] = (jnp.where(head0, alphas[0], alphas[1]) * acc_sc[c]
                             + jnp.where(head0, pvs[0], pvs[1]))

    def body(j, carry):
        update((2 * j, 2 * j + 1), False)
        return carry

    lax.fori_loop(0, qi // 2, body, 0)

    @pl.when(qi % 2 == 1)
    def _():
        update((qi - 1,), False)

    update((qi,), True)
    for c in range(nsplit):
        o_t = acc_sc[c] / jnp.where(head0, l_sc[2 * c], l_sc[2 * c + 1])
        o_ref[c * tc:(c + 1) * tc, :] = o_t.T.astype(o_ref.dtype)


def prefill_attention(q, kfull, vt, bs, seq):
    tk = vt.shape[2]
    tq = tk
    nq = seq // tq
    n = bs * seq
    nsplit = 2 if tq % (2 * LANES) == 0 else 1
    tc = tq // nsplit
    return pl.pallas_call(
        _prefill_attn_kernel,
        grid=(bs, MLA_HEADS // 2, nq),
        in_specs=[
            pl.BlockSpec((tq, 2 * HEAD_PAD), lambda b, hp, qi: (b * nq + qi, hp)),
            pl.BlockSpec((seq, 2 * HEAD_PAD), lambda b, hp, qi: (b, hp)),
            pl.BlockSpec((nq, 2 * V_HEAD, tk), lambda b, hp, qi: (b, hp, 0)),
        ],
        out_specs=pl.BlockSpec((tq, 2 * V_HEAD), lambda b, hp, qi: (b * nq + qi, hp)),
        out_shape=jax.ShapeDtypeStruct((n, MLA_HEADS * V_HEAD), BF16),
        scratch_shapes=[pltpu.VMEM((2 * nsplit, 1, tc), F32), pltpu.VMEM((2 * nsplit, 1, tc), F32),
                        pltpu.VMEM((nsplit, 2 * V_HEAD, tc), F32)],
        compiler_params=_cparams(("parallel", "parallel", "arbitrary")),
        name="prefill_attention",
    )(q, kfull, vt)


def _decode_attn_kernel(pt_ref, qa_ref, q_ref, cnew_ref, rnew_ref, kv_hbm, kr_hbm, o_ref,
                        kv_buf, kr_buf, kv_sem, kr_sem, *, layer, n_pages):
    b = pl.program_id(0)
    nb = pl.num_programs(0)
    seq = qa_ref.shape[0]
    rows = MLA_HEADS * seq
    page = kv_hbm.shape[2]
    slot = b % 2

    def page_copies(seq_idx, dst_slot, i):
        pid = pt_ref[seq_idx, i]
        return (pltpu.make_async_copy(kv_hbm.at[layer, pid],
                                      kv_buf.at[dst_slot, pl.ds(i * page, page), :],
                                      kv_sem.at[dst_slot]),
                pltpu.make_async_copy(kr_hbm.at[layer, pid],
                                      kr_buf.at[dst_slot, :, pl.ds(i * page, page)],
                                      kr_sem.at[dst_slot]))

    def start_fetch(seq_idx, dst_slot):
        for i in range(n_pages):
            for cp in page_copies(seq_idx, dst_slot, i):
                cp.start()

    @pl.when(b == 0)
    def _():
        start_fetch(0, 0)

    @pl.when(b + 1 < nb)
    def _():
        start_fetch(b + 1, 1 - slot)

    qa = jnp.concatenate([qa_ref[:, h * KV_LORA:(h + 1) * KV_LORA] for h in range(MLA_HEADS)],
                         axis=0) * SCORE_SCALE
    qr = jnp.concatenate([q_ref[:, h * HEAD_PAD:(h + 1) * HEAD_PAD] for h in range(MLA_HEADS)],
                         axis=0)[:, QK_NOPE:QK_NOPE + QK_ROPE] * SCORE_SCALE
    qa = qa.astype(BF16)
    qr = qr.astype(BF16)

    for i in range(n_pages):
        for cp in page_copies(b, slot, i):
            cp.wait()

    cp_pages = min(n_pages, DECODE_CHUNK_PAGES)
    ck = cp_pages * page
    chunks, scores = [], []
    for st in range(0, n_pages * page, ck):
        c = kv_buf[slot, pl.ds(st, ck), :].astype(BF16)
        r = kr_buf[slot, :, pl.ds(st, ck)].astype(BF16)
        chunks.append(c)
        scores.append(_dot_nt(qa, c) + _dot(qr, r))
    pad = page - seq
    c_new = jnp.concatenate([cnew_ref[...], jnp.zeros((pad, KV_LORA), F32)], axis=0).astype(BF16)
    r_new = jnp.concatenate([rnew_ref[...], jnp.zeros((pad, QK_ROPE), F32)], axis=0).astype(BF16)
    s_new = _dot_nt(qa, c_new) + _dot_nt(qr, r_new)
    tok = lax.broadcasted_iota(jnp.int32, (rows, page), 0) % seq
    key = lax.broadcasted_iota(jnp.int32, (rows, page), 1)
    chunks.append(c_new)
    scores.append(jnp.where(key <= tok, s_new, NEG_BIG))

    m = jnp.max(scores[0], axis=1, keepdims=True)
    for s in scores[1:]:
        m = jnp.maximum(m, jnp.max(s, axis=1, keepdims=True))
    l = jnp.zeros((rows, 1), F32)
    acc = jnp.zeros((rows, KV_LORA), F32)
    for c, s in zip(chunks, scores):
        p = jnp.exp2(s - m)
        l = l + jnp.sum(p, axis=1, keepdims=True)
        acc = acc + _dot(p.astype(BF16), c)
    o = acc / l
    for h in range(MLA_HEADS):
        o_ref[:, h * KV_LORA:(h + 1) * KV_LORA] = o[h * seq:(h + 1) * seq, :].astype(o_ref.dtype)


def decode_attention(qa, q, ckv, kr, cache_kv, cache_kr_t, page_table, layer, bs, seq):
    n_pages = page_table.shape[1]
    page = cache_kv.shape[2]
    n = bs * seq
    grid_spec = pltpu.PrefetchScalarGridSpec(
        num_scalar_prefetch=1,
        grid=(bs,),
        in_specs=[
            pl.BlockSpec((seq, MLA_HEADS * KV_LORA), lambda b, pt: (b, 0)),
            pl.BlockSpec((seq, MLA_HEADS * HEAD_PAD), lambda b, pt: (b, 0)),
            pl.BlockSpec((seq, KV_LORA), lambda b, pt: (b, 0)),
            pl.BlockSpec((seq, QK_ROPE), lambda b, pt: (b, 0)),
            pl.BlockSpec(memory_space=pl.ANY),
            pl.BlockSpec(memory_space=pl.ANY),
        ],
        out_specs=pl.BlockSpec((seq, MLA_HEADS * KV_LORA), lambda b, pt: (b, 0)),
        scratch_shapes=[pltpu.VMEM((2, n_pages * page, KV_LORA), F32),
                        pltpu.VMEM((2, QK_ROPE, n_pages * page), F32),
                        pltpu.SemaphoreType.DMA((2,)), pltpu.SemaphoreType.DMA((2,))],
    )
    return pl.pallas_call(
        functools.partial(_decode_attn_kernel, layer=layer, n_pages=n_pages),
        grid_spec=grid_spec,
        out_shape=jax.ShapeDtypeStruct((n, MLA_HEADS * KV_LORA), F32),
        compiler_params=_cparams(("arbitrary",)),
        name="decode_attention",
    )(page_table, qa, q, ckv, kr, cache_kv, cache_kr_t)


def _merge_kernel(x_ref, ys_ref, ym_ref, gs_ref, gm_ref, ws_ref, wm_ref, wo_ref, o_ref):
    ms = _dot(ys_ref[...].astype(BF16), ws_ref[...])
    mm = _dot(ym_ref[...].astype(BF16), wm_ref[...])
    merged = jax.nn.sigmoid(gs_ref[...].astype(F32)) * ms + jax.nn.sigmoid(gm_ref[...].astype(F32)) * mm
    o_ref[...] = x_ref[...] + _dot(merged.astype(BF16), wo_ref[...])


def merge(x, y_ssm, y_mla, proj, w, layer):
    n, d = x.shape
    tm = _pick(n, (512, 256, 128))
    full = lambda a: _resident_layer(a.shape, layer)
    return pl.pallas_call(
        _merge_kernel,
        grid=(n // tm,),
        in_specs=[
            pl.BlockSpec((tm, d), lambda i: (i, 0)),
            pl.BlockSpec((tm, D_INNER), lambda i: (i, 0)),
            pl.BlockSpec((tm, MLA_HEADS * V_HEAD), lambda i: (i, 0)),
            pl.BlockSpec((tm, d), lambda i: (i, C_GS // D_MODEL)),
            pl.BlockSpec((tm, d), lambda i: (i, C_GM // D_MODEL)),
            full(w["w_br_ssm"]), full(w["w_br_mla"]), full(w["w_out"]),
        ],
        out_specs=pl.BlockSpec((tm, d), lambda i: (i, 0)),
        out_shape=jax.ShapeDtypeStruct((n, d), F32),
        compiler_params=_cparams(("parallel",)),
        name="merge",
    )(x, y_ssm, y_mla, proj, proj, w["w_br_ssm"], w["w_br_mla"], w["w_out"])


def _ple_kernel(x_ref, p_ref, g_ref, wg_ref, wp_ref, fg_ref, o_ref, *, final):
    x = x_ref[...]
    gate = jax.nn.sigmoid(_dot(_rms(x, g_ref[...]).astype(BF16), wg_ref[...]))
    x = x + gate * _dot(p_ref[...].astype(BF16), wp_ref[...])
    if final:
        x = _rms(x, fg_ref[...])
    o_ref[...] = x


def ple(x, p, w, big_w, layer, final_norm, final):
    n, d = x.shape
    tm = _pick(n, (512, 256, 128))
    full = lambda a: _resident_layer(a.shape, layer)
    return pl.pallas_call(
        functools.partial(_ple_kernel, final=final),
        grid=(n // tm,),
        in_specs=[
            pl.BlockSpec((tm, d), lambda i: (i, 0)),
            pl.BlockSpec((tm, PLE_DIM), lambda i: (i, 0)),
            pl.BlockSpec((1, d), lambda i: (0, 0)),
            full(big_w["w_ple_gate"]), full(big_w["w_ple_proj"]),
            pl.BlockSpec((1, d), lambda i: (0, 0)),
        ],
        out_specs=pl.BlockSpec((tm, d), lambda i: (i, 0)),
        out_shape=jax.ShapeDtypeStruct((n, d), F32),
        compiler_params=_cparams(("parallel",)),
        name="ple",
    )(x, p, w["ple_norm"], big_w["w_ple_gate"], big_w["w_ple_proj"], final_norm)


def _rope_tables(pos, rows):
    half = QK_ROPE // 2
    inv = ROPE_BASE ** (-jnp.arange(half, dtype=F32) / half)
    ang = pos.astype(F32)[:, None] * inv[None, :]
    cos, sin = jnp.cos(ang), jnp.sin(ang)
    z = lambda w_: jnp.zeros((pos.shape[0], w_), F32)
    cos_t = jnp.concatenate([jnp.ones((pos.shape[0], QK_NOPE), F32), cos, cos, z(32)], axis=1)
    sin_hi = jnp.concatenate([z(QK_NOPE + half), sin, z(32)], axis=1)
    sin_lo = jnp.concatenate([z(QK_NOPE), -sin, z(half + 32)], axis=1)
    reps = max(1, rows // pos.shape[0])
    return tuple(jnp.tile(t, (reps, 1)) for t in (cos_t, sin_hi, sin_lo))


def _layer_weights(i, p):
    row = lambda v: v.reshape(1, -1).astype(F32)
    w_in = p["w_in"][i]
    off_z, off_xbc, off_dt, off_q, off_kv = 2048, 5120, 5152, 5664, 5952
    z = w_in[:, :off_z]
    xbc = w_in[:, off_z:off_xbc]
    dt = w_in[:, off_xbc:off_dt]
    q_lat = w_in[:, off_dt:off_q]
    kv_lat = w_in[:, off_q:off_q + KV_LORA]
    k_rope = w_in[:, off_q + KV_LORA:off_kv]
    gates = w_in[:, off_kv:]
    w_main = jnp.concatenate([z, xbc, gates, q_lat], axis=1).astype(BF16)
    w_main_x = jnp.concatenate([w_main, jnp.zeros((D_MODEL, C_DTX - MAIN_DIM), BF16),
                                jnp.repeat(dt, SSM_HEAD_DIM, axis=1).astype(BF16)], axis=1)
    zeros32 = jnp.zeros((D_MODEL, 32), F32)
    small = jnp.concatenate([kv_lat, dt, zeros32, k_rope, zeros32], axis=1)

    lane_pad = lambda v: jnp.pad(v.astype(F32), (0, LANES - v.shape[0])).reshape(1, LANES)
    a_neg = -jnp.exp(p["a_log"][i].astype(F32))

    w_qb = p["w_qb"][i].reshape(Q_LORA, MLA_HEADS, QK_NOPE + QK_ROPE)
    w_qb_pad = jnp.pad(w_qb, ((0, 0), (0, 0), (0, HEAD_PAD - QK_NOPE - QK_ROPE)))
    w_uk = p["w_uk"][i]
    wk_nope = jnp.pad(w_uk, ((0, 0), (0, 0), (0, HEAD_PAD - QK_NOPE))).reshape(KV_LORA, -1)
    place = jnp.zeros((LANES, MLA_HEADS, HEAD_PAD), F32)
    j = jnp.arange(QK_ROPE)
    place = place.at[S_ROPE - KV_LORA + j, :, QK_NOPE + j].set(1.0).reshape(LANES, -1)
    w_abs = jnp.pad(jnp.transpose(w_uk, (1, 2, 0)), ((0, 0), (0, HEAD_PAD - QK_NOPE), (0, 0)))
    w_uv = p["w_uv"][i]
    eye = jnp.eye(MLA_HEADS, dtype=F32)
    w_uv_bd = (jnp.transpose(w_uv, (1, 0, 2))[:, :, None, :] * eye[:, None, :, None]).reshape(
        MLA_HEADS * KV_LORA, MLA_HEADS * V_HEAD)

    return {
        "ffn1_norm": row(p["ffn1_norm"][i]), "ffn2_norm": row(p["ffn2_norm"][i]),
        "mix_norm": row(p["mix_norm"][i]),
        "w_main": w_main, "w_main_x": w_main_x,
        "w_small": small.astype(BF16),
        "conv_w": p["conv_w"][i].astype(F32), "conv_b": row(p["conv_b"][i]),
        "dt_bias_l": lane_pad(p["dt_bias"][i]), "a_neg_l": lane_pad(a_neg),
        "dt_bias_x": row(jnp.repeat(p["dt_bias"][i], SSM_HEAD_DIM)),
        "a_neg_x": row(jnp.repeat(a_neg, SSM_HEAD_DIM)),
        "d_skip_x": row(jnp.repeat(p["d_skip"][i], SSM_HEAD_DIM)),
        "ssm_norm": row(p["ssm_norm"][i]),
        "q_norm": row(p["q_norm"][i]), "kv_norm": row(p["kv_norm"][i]),
        "w_qb_pad": w_qb_pad.reshape(Q_LORA, -1).astype(BF16),
        "wk_full": jnp.concatenate([wk_nope, place], axis=0).astype(BF16),
        "w_uv_t": w_uv.reshape(KV_LORA, -1).T.astype(BF16),
        "w_abs": w_abs.astype(BF16),
        "w_uv_bd": w_uv_bd.astype(BF16),
        "ple_norm": row(p["ple_norm"][i]),
    }


def _trunk_layer(x, p_l, tabs, conv_prev, ssm_prev, w, ffn_w, layer, bs, seq, final_norm, final, paged):
    x = ffn_half_step(x, w["ffn1_norm"], ffn_w["ffn1_w_gu"], ffn_w["ffn1_w_down"], layer)
    if paged is None:
        proj, small = in_project(x, w["mix_norm"], w["w_main"], w["w_small"], BF16, (512, 256, 128))
    else:
        proj, small = in_project(x, w["mix_norm"], w["w_main_x"], w["w_small"], F32, (256, 128))
    if paged is None:
        y_ssm, new_conv, new_ssm = ssm_prefill(proj, small, conv_prev, ssm_prev, w, bs, seq)
        c_kv, k_rope, kfull, vt = kv_project(small, w, tabs, True)
        q = q_project(proj, w, tabs, False)
        y_mla = prefill_attention(q, kfull, vt, bs, seq)
    else:
        cache_kv, cache_kr_t, page_table, new_ssm_stack = paged
        y_ssm, new_conv, new_ssm = ssm_step(proj, conv_prev, ssm_prev, new_ssm_stack, layer, w, bs, seq)
        c_kv, k_rope = kv_project(small, w, tabs, False)
        q, qa = q_project(proj, w, tabs, True)
        o_lat = decode_attention(qa, q, c_kv, k_rope, cache_kv, cache_kr_t, page_table, layer, bs, seq)
        y_mla = norm_matmul(o_lat, None, w["w_uv_bd"], BF16, norm=False, tm_prefs=(256, 128))
    x = merge(x, y_ssm, y_mla, proj, ffn_w, layer)
    x = ffn_half_step(x, w["ffn2_norm"], ffn_w["ffn2_w_gu"], ffn_w["ffn2_w_down"], layer)
    x = ple(x, p_l, w, ffn_w, layer, final_norm, final)
    return x, c_kv, k_rope, new_conv, new_ssm


def kernel(x_prompt, x_sample, cache_kv, cache_kr, state_conv, state_ssm, page_table, p_prompt, p_sample,
           ffn1_norm, ffn1_w_gu, ffn1_w_down, mix_norm, w_in, conv_w, conv_b, dt_bias, a_log, d_skip,
           ssm_norm, q_norm, w_qb, kv_norm, w_uk, w_uv, w_br_ssm, w_br_mla, w_out,
           ffn2_norm, ffn2_w_gu, ffn2_w_down, ple_norm, w_ple_gate, w_ple_proj, final_norm):
    params = dict(ffn1_norm=ffn1_norm, ffn1_w_gu=ffn1_w_gu, ffn1_w_down=ffn1_w_down, mix_norm=mix_norm,
                  w_in=w_in, conv_w=conv_w, conv_b=conv_b, dt_bias=dt_bias, a_log=a_log, d_skip=d_skip,
                  ssm_norm=ssm_norm, q_norm=q_norm, w_qb=w_qb, kv_norm=kv_norm, w_uk=w_uk, w_uv=w_uv,
                  w_br_ssm=w_br_ssm, w_br_mla=w_br_mla, w_out=w_out, ffn2_norm=ffn2_norm,
                  ffn2_w_gu=ffn2_w_gu, ffn2_w_down=ffn2_w_down, ple_norm=ple_norm,
                  w_ple_gate=w_ple_gate, w_ple_proj=w_ple_proj)
    depth = w_in.shape[0]
    bp, sp, d = x_prompt.shape
    bsm, ss, _ = x_sample.shape
    past_len = page_table.shape[1] * cache_kv.shape[2]
    np_, ns = bp * sp, bsm * ss

    tabs_p = _rope_tables(jnp.arange(sp), sp)
    tabs_s = _rope_tables(past_len + jnp.arange(ss), _pick(ns, (256, 128)))
    conv0 = jnp.zeros((bp, D_CONV - 1, CONV_DIM), F32)
    ssm0 = jnp.zeros((bp, SSM_HEADS, SSM_HEAD_DIM, SSM_STATE), F32)
    fnorm = final_norm.reshape(1, -1).astype(F32)

    cache_kr_t = jnp.swapaxes(cache_kr, 2, 3)
    ffn_w = {name: cast_bf16(params[name])
             for name in ("ffn1_w_gu", "ffn1_w_down", "ffn2_w_gu", "ffn2_w_down", "w_br_ssm",
                          "w_br_mla", "w_out", "w_ple_gate", "w_ple_proj")}

    hp = x_prompt.reshape(np_, d)
    hs = x_sample.reshape(ns, d)
    outs_p, outs_s = [], []
    new_ssm_stack = None
    for i in range(depth):
        last = i == depth - 1
        w = _layer_weights(i, params)
        hs, *rest_s = _trunk_layer(hs, p_sample[i].reshape(ns, PLE_DIM), tabs_s, state_conv,
                                   state_ssm, w, ffn_w, i, bsm, ss, fnorm, last,
                                   (cache_kv, cache_kr_t, page_table, new_ssm_stack))
        hp, *rest_p = _trunk_layer(hp, p_prompt[i].reshape(np_, PLE_DIM), tabs_p, conv0, ssm0, w,
                                   ffn_w, i, bp, sp, fnorm, last, None)
        new_ssm_stack = rest_s[3]
        outs_p.append(rest_p)
        outs_s.append(rest_s)

    stack = lambda outs, k, shape: jnp.stack([o[k] for o in outs]).reshape((depth,) + shape)
    return (
        hp.reshape(bp, sp, d), hs.reshape(bsm, ss, d),
        stack(outs_p, 0, (bp, sp, KV_LORA)), stack(outs_p, 1, (bp, sp, QK_ROPE)),
        stack(outs_p, 2, (bp, D_CONV - 1, CONV_DIM)),
        stack(outs_p, 3, (bp, SSM_HEADS, SSM_HEAD_DIM, SSM_STATE)),
        stack(outs_s, 0, (bsm, ss, KV_LORA)), stack(outs_s, 1, (bsm, ss, QK_ROPE)),
        stack(outs_s, 2, (bsm, D_CONV - 1, CONV_DIM)),
        new_ssm_stack,
    )
```

```python
import functools
import math

import jax
import jax.numpy as jnp
from jax import lax
from jax.experimental import pallas as pl
from jax.experimental.pallas import tpu as pltpu

F32 = jnp.float32
BF16 = jnp.bfloat16

D_MODEL = 1024
D_FF = 2816
D_INNER = 2048
SSM_HEAD_DIM = 64
SSM_HEADS = 32
SSM_GROUPS = 4
HEADS_PER_GROUP = SSM_HEADS // SSM_GROUPS
SSM_STATE = 128
GROUP_CH = D_INNER // SSM_GROUPS
D_CONV = 4
CONV_DIM = D_INNER + 2 * SSM_GROUPS * SSM_STATE
SSD_CHUNK = 128
MLA_HEADS = 16
Q_LORA = 512
KV_LORA = 256
QK_NOPE = 64
QK_ROPE = 32
V_HEAD = 64
ROPE_BASE = 10000.0
ATTN_SCALE = (QK_NOPE + QK_ROPE) ** -0.5
SCORE_SCALE = ATTN_SCALE * math.log2(math.e)
PLE_DIM = 256
EPS = 1e-6
LANES = 128
HEAD_PAD = 128
DECODE_CHUNK_PAGES = 4
NEG_BIG = -1e30

C_Z, C_XS, C_BC, C_GS, C_GM, C_Q = 0, 2048, 4096, 5120, 6144, 7168
MAIN_DIM = 7680
C_DTX = 8192
MAIN_DIM_X = C_DTX + D_INNER
SMALL_DIM = 384
S_DT, S_ROPE = 256, 320

VMEM_LIMIT = 56 * 1024 * 1024


def _cparams(sem):
    return pltpu.CompilerParams(dimension_semantics=sem, vmem_limit_bytes=VMEM_LIMIT)


def _pick(n, prefs):
    for p in prefs:
        if n % p == 0:
            return p
    return n


def _dot(a, b):
    return jnp.dot(a, b, preferred_element_type=F32)


def _dot_nt(a, b):
    return lax.dot_general(a, b, (((1,), (1,)), ((), ())), preferred_element_type=F32)


def _silu(x):
    return x * jax.nn.sigmoid(x)


def _softplus(x):
    return jnp.maximum(x, 0.0) + jnp.log1p(jnp.exp(-jnp.abs(x)))


def _rms(x, g):
    return x * lax.rsqrt(jnp.mean(x * x, axis=-1, keepdims=True) + EPS) * g


def _nm_kernel(x_ref, g_ref, w_ref, o_ref, u_sc, *, norm):
    @pl.when(pl.program_id(1) == 0)
    def _():
        x = x_ref[...].astype(F32)
        if norm:
            x = _rms(x, g_ref[...])
        u_sc[...] = x.astype(BF16)

    o_ref[...] = _dot(u_sc[...], w_ref[...]).astype(o_ref.dtype)


def norm_matmul(x, g, w, out_dtype, *, norm=True, col_block=0, tm_prefs=(1024, 512, 256, 128)):
    n = x.shape[0]
    k, nout = w.shape
    tm = _pick(n, tm_prefs)
    tn = _pick(nout, (1280, 1024, 512, 384, 256, 128))
    if not norm:
        g = jnp.ones((1, k), F32)
    return pl.pallas_call(
        functools.partial(_nm_kernel, norm=norm),
        grid=(n // tm, nout // tn),
        in_specs=[
            pl.BlockSpec((tm, k), lambda i, j: (i, col_block)),
            pl.BlockSpec((1, k), lambda i, j: (0, 0)),
            pl.BlockSpec((k, tn), lambda i, j: (0, j)),
        ],
        out_specs=pl.BlockSpec((tm, tn), lambda i, j: (i, j)),
        out_shape=jax.ShapeDtypeStruct((n, nout), out_dtype),
        scratch_shapes=[pltpu.VMEM((tm, k), BF16)],
        compiler_params=_cparams(("parallel", "arbitrary")),
        name="norm_matmul",
    )(x, g, w)


def _resident(shape):
    return pl.BlockSpec(shape, lambda *_: (0,) * len(shape), pipeline_mode=pl.Buffered(1))


def _resident_layer(stacked_shape, layer):
    return pl.BlockSpec((None,) + tuple(stacked_shape[1:]),
                        lambda *_: (layer,) + (0,) * (len(stacked_shape) - 1),
                        pipeline_mode=pl.Buffered(1))


def _cast_kernel(x_ref, o_ref):
    o_ref[...] = x_ref[...].astype(o_ref.dtype)


def cast_bf16(x):
    depth, rows, cols = x.shape
    tr = _pick(rows, (256, 128))
    return pl.pallas_call(
        _cast_kernel,
        grid=(depth, rows // tr),
        in_specs=[pl.BlockSpec((1, tr, cols), lambda l, i: (l, i, 0))],
        out_specs=pl.BlockSpec((1, tr, cols), lambda l, i: (l, i, 0)),
        out_shape=jax.ShapeDtypeStruct(x.shape, BF16),
        compiler_params=_cparams(("parallel", "parallel")),
        name="cast_bf16",
    )(x)


def _in_proj_kernel(x_ref, g_ref, wm_ref, ws_ref, om_ref, os_ref, *, chunk):
    u = _rms(x_ref[...], g_ref[...]).astype(BF16)
    for c0 in range(0, wm_ref.shape[1], chunk):
        om_ref[:, c0:c0 + chunk] = _dot(u, wm_ref[:, c0:c0 + chunk]).astype(om_ref.dtype)
    os_ref[...] = _dot(u, ws_ref[...])


def in_project(x, g, w_main, w_small, main_dtype, tm_prefs):
    n, d = x.shape
    nmain = w_main.shape[1]
    tm = _pick(n, tm_prefs)
    return pl.pallas_call(
        functools.partial(_in_proj_kernel, chunk=_pick(nmain, (1280, 1024, 512))),
        grid=(n // tm,),
        in_specs=[
            pl.BlockSpec((tm, d), lambda i: (i, 0)),
            pl.BlockSpec((1, d), lambda i: (0, 0)),
            _resident(w_main.shape),
            _resident(w_small.shape),
        ],
        out_specs=[pl.BlockSpec((tm, nmain), lambda i: (i, 0)),
                   pl.BlockSpec((tm, SMALL_DIM), lambda i: (i, 0))],
        out_shape=[jax.ShapeDtypeStruct((n, nmain), main_dtype),
                   jax.ShapeDtypeStruct((n, SMALL_DIM), F32)],
        compiler_params=_cparams(("parallel",)),
        name="in_project",
    )(x, g, w_main, w_small)


def _ffn_kernel(x_ref, g_ref, wgu_ref, wd_ref, o_ref):
    x = x_ref[...]
    dff = wd_ref.shape[0]
    u = _rms(x, g_ref[...]).astype(BF16)
    a = _dot(u, wgu_ref[:, :dff])
    b = _dot(u, wgu_ref[:, dff:])
    h = (_silu(a) * b).astype(BF16)
    o_ref[...] = x + 0.5 * _dot(h, wd_ref[...])


def ffn_half_step(x, g, w_gu, w_down, layer):
    n, d = x.shape
    tm = _pick(n, (512, 256, 128))
    return pl.pallas_call(
        _ffn_kernel,
        grid=(n // tm,),
        in_specs=[
            pl.BlockSpec((tm, d), lambda i: (i, 0)),
            pl.BlockSpec((1, d), lambda i: (0, 0)),
            _resident_layer(w_gu.shape, layer),
            _resident_layer(w_down.shape, layer),
        ],
        out_specs=pl.BlockSpec((tm, d), lambda i: (i, 0)),
        out_shape=jax.ShapeDtypeStruct((n, d), F32),
        compiler_params=_cparams(("parallel",)),
        name="ffn_half_step",
    )(x, g, w_gu, w_down)


def _causal_conv(xpad_sc, rows, w_ref, b_ref):
    w = w_ref[...]
    win = xpad_sc[0:8 + rows, :]
    acc = win[8:, :] * w[D_CONV - 1:D_CONV, :]
    for k in range(D_CONV - 1):
        shifted = pltpu.roll(win, D_CONV - 1 - k, axis=0)[8:, :]
        acc = acc + shifted * w[k:k + 1, :]
    return _silu(acc + b_ref[...])


def _gate_and_group_norm(y, z, norm_w):
    y = y * _silu(z)
    parts = []
    for g in range(SSM_GROUPS):
        yg = y[:, g * GROUP_CH:(g + 1) * GROUP_CH]
        parts.append(yg * lax.rsqrt(jnp.mean(yg * yg, axis=-1, keepdims=True) + EPS))
    return jnp.concatenate(parts, axis=1) * norm_w


def _cumsum_rows(a):
    rows = a.shape[0]
    row = lax.broadcasted_iota(jnp.int32, a.shape, 0)
    s = 1
    while s < rows:
        a = a + jnp.where(row >= s, pltpu.roll(a, s, axis=0), 0.0)
        s *= 2
    return a


def _ssm_prefill_kernel(z_ref, xs_ref, bc_ref, dt_ref, cprev_ref, sprev_ref, convw_ref, convb_ref,
                        dtb_ref, aneg_ref, dskip_ref, norm_ref,
                        y_ref, nconv_ref, nssm_ref, xpad_sc, h_sc):
    c = pl.program_id(1)
    t = SSD_CHUNK
    hp = SSM_HEAD_DIM

    @pl.when(c == 0)
    def _():
        xpad_sc[0:8, :] = jnp.zeros((8, CONV_DIM), F32)
        xpad_sc[5:8, :] = cprev_ref[0]
        h_sc[...] = sprev_ref[0].reshape(D_INNER, SSM_STATE)

    xpad_sc[8:8 + t, 0:D_INNER] = xs_ref[...].astype(F32)
    xpad_sc[8:8 + t, D_INNER:CONV_DIM] = bc_ref[...].astype(F32)
    conv = _causal_conv(xpad_sc, t, convw_ref, convb_ref)
    tail = xpad_sc[8 + t - 3:8 + t, :]
    xpad_sc[5:8, :] = tail

    xs = conv[:, :D_INNER]
    gs = SSM_GROUPS * SSM_STATE
    bm = conv[:, D_INNER:D_INNER + gs]
    cm = conv[:, D_INNER + gs:]

    dt = _softplus(dt_ref[...] + dtb_ref[...])
    acum = _cumsum_rows(dt * aneg_ref[...])
    alast = acum[t - 1:t, :]
    acum_t = acum.T
    dt_t = dt.T
    wst_t = (dt * jnp.exp2(alast - acum)).T
    xs_t = xs.T

    def row_bcast(a, r, rows):
        one = jnp.broadcast_to(a[r:r + 1, :], (8, a.shape[1]))
        return jnp.concatenate([one] * (rows // 8), axis=0)

    tri = (lax.broadcasted_iota(jnp.int32, (t, t), 0) >= lax.broadcasted_iota(jnp.int32, (t, t), 1))
    low_half = lax.broadcasted_iota(jnp.int32, (t, 2 * hp), 1) < hp

    y_parts = []
    for g in range(SSM_GROUPS):
        bg = bm[:, g * SSM_STATE:(g + 1) * SSM_STATE]
        cg = cm[:, g * SSM_STATE:(g + 1) * SSM_STATE]
        cb = _dot_nt(cg.astype(BF16), bg.astype(BF16))
        for pair in range(HEADS_PER_GROUP // 2):
            h0 = g * HEADS_PER_GROUP + 2 * pair
            lhs = []
            for h in (h0, h0 + 1):
                colb = jnp.broadcast_to(acum[:, h:h + 1], (t, t))
                decay = jnp.where(tri, jnp.exp2(colb - row_bcast(acum_t, h, t)), 0.0)
                m = cb * decay * row_bcast(dt_t, h, t)
                ce = cg * jnp.exp2(colb)
                lhs.append(jnp.concatenate([m, ce], axis=1))
            lhs = jnp.concatenate(lhs, axis=0).astype(BF16)
            r0 = h0 * hp
            rhs_t = jnp.concatenate([xs_t[r0:r0 + 2 * hp, :], h_sc[r0:r0 + 2 * hp, :]],
                                    axis=1).astype(BF16)
            out = _dot_nt(lhs, rhs_t)
            y_parts.append(jnp.where(low_half, out[:t, :], out[t:, :]))
    y = jnp.concatenate(y_parts, axis=1) + dskip_ref[...] * xs
    y_ref[...] = _gate_and_group_norm(y, z_ref[...].astype(F32), norm_ref[...]).astype(y_ref.dtype)

    elast = jnp.exp2(alast)
    for g in range(SSM_GROUPS):
        bg = bm[:, g * SSM_STATE:(g + 1) * SSM_STATE].astype(BF16)
        xw = []
        for hh in range(HEADS_PER_GROUP):
            h = g * HEADS_PER_GROUP + hh
            xw.append(xs_t[h * hp:(h + 1) * hp, :] * row_bcast(wst_t, h, hp))
        dh = _dot(jnp.concatenate(xw, axis=0).astype(BF16), bg)
        for hh in range(HEADS_PER_GROUP):
            h = g * HEADS_PER_GROUP + hh
            rows = slice(h * hp, (h + 1) * hp)
            h_sc[rows, :] = h_sc[rows, :] * elast[:, h:h + 1] + dh[hh * hp:(hh + 1) * hp, :]

    @pl.when(c == pl.num_programs(1) - 1)
    def _():
        nconv_ref[0] = tail
        nssm_ref[0] = h_sc[...].reshape(SSM_HEADS, SSM_HEAD_DIM, SSM_STATE)


def ssm_prefill(proj, small, conv_prev, ssm_prev, w, bs, seq):
    t = SSD_CHUNK
    nc = seq // t
    n = bs * seq
    row = lambda b, c: b * nc + c
    vec = lambda width: pl.BlockSpec((1, width), lambda b, c: (0, 0))
    return pl.pallas_call(
        _ssm_prefill_kernel,
        grid=(bs, nc),
        in_specs=[
            pl.BlockSpec((t, D_INNER), lambda b, c: (row(b, c), C_Z // D_INNER)),
            pl.BlockSpec((t, D_INNER), lambda b, c: (row(b, c), C_XS // D_INNER)),
            pl.BlockSpec((t, 1024), lambda b, c: (row(b, c), C_BC // 1024)),
            pl.BlockSpec((t, LANES), lambda b, c: (row(b, c), S_DT // LANES)),
            pl.BlockSpec((1, D_CONV - 1, CONV_DIM), lambda b, c: (b, 0, 0)),
            pl.BlockSpec((1, SSM_HEADS, SSM_HEAD_DIM, SSM_STATE), lambda b, c: (b, 0, 0, 0)),
            pl.BlockSpec((D_CONV, CONV_DIM), lambda b, c: (0, 0)),
            vec(CONV_DIM), vec(LANES), vec(LANES), vec(D_INNER), vec(D_INNER),
        ],
        out_specs=[
            pl.BlockSpec((t, D_INNER), lambda b, c: (row(b, c), 0)),
            pl.BlockSpec((1, D_CONV - 1, CONV_DIM), lambda b, c: (b, 0, 0)),
            pl.BlockSpec((1, SSM_HEADS, SSM_HEAD_DIM, SSM_STATE), lambda b, c: (b, 0, 0, 0)),
        ],
        out_shape=[
            jax.ShapeDtypeStruct((n, D_INNER), F32),
            jax.ShapeDtypeStruct((bs, D_CONV - 1, CONV_DIM), F32),
            jax.ShapeDtypeStruct((bs, SSM_HEADS, SSM_HEAD_DIM, SSM_STATE), F32),
        ],
        scratch_shapes=[pltpu.VMEM((t + 8, CONV_DIM), F32), pltpu.VMEM((D_INNER, SSM_STATE), F32)],
        compiler_params=_cparams(("parallel", "arbitrary")),
        name="ssm_prefill",
    )(proj, proj, proj, small, conv_prev, ssm_prev, w["conv_w"], w["conv_b"], w["dt_bias_l"],
      w["a_neg_l"], w["d_skip_x"], w["ssm_norm"])


def _ssm_step_kernel(z_ref, xs_ref, bc_ref, dtx_ref, cprev_ref, sprev_ref, convw_ref, convb_ref,
                     dtbx_ref, anegx_ref, dskip_ref, norm_ref, *rest):
    y_ref, nconv_ref, nssm_ref, xpad_sc, xw_sc, bpad_sc = rest[-6:]
    seq = xs_ref.shape[0]
    hp = SSM_HEAD_DIM
    gs = SSM_GROUPS * SSM_STATE

    @pl.when(pl.program_id(0) == 0)
    def _():
        xw_sc[...] = jnp.zeros_like(xw_sc)
        bpad_sc[...] = jnp.zeros_like(bpad_sc)
        xpad_sc[0:8, :] = jnp.zeros((8, CONV_DIM), F32)

    xpad_sc[5:8, :] = cprev_ref[0, 0]
    xpad_sc[8:8 + seq, 0:D_INNER] = xs_ref[...]
    xpad_sc[8:8 + seq, D_INNER:CONV_DIM] = bc_ref[...]
    conv = _causal_conv(xpad_sc, seq, convw_ref, convb_ref)
    nconv_ref[0] = xpad_sc[8 + seq - 3:8 + seq, :]

    xs = conv[:, :D_INNER]
    bm = conv[:, D_INNER:D_INNER + gs]
    cm = conv[:, D_INNER + gs:]

    dtx = _softplus(dtx_ref[...] + dtbx_ref[...])
    acum = _cumsum_rows(dtx * anegx_ref[...])
    alast = acum[seq - 1:seq, :]
    xd = xs * dtx

    c_all = jnp.concatenate([cm[:, g * SSM_STATE:(g + 1) * SSM_STATE] for g in range(SSM_GROUPS)],
                            axis=0).astype(BF16)
    y_off = []
    for g in range(SSM_GROUPS):
        hg = sprev_ref[0, 0, g * HEADS_PER_GROUP:(g + 1) * HEADS_PER_GROUP].reshape(GROUP_CH, SSM_STATE)
        y_off.append(_dot_nt(c_all, hg.astype(BF16))[g * seq:(g + 1) * seq, :])
    y = jnp.concatenate(y_off, axis=1) * jnp.exp(acum)

    ones = jnp.ones((SSM_STATE, GROUP_CH), BF16)
    cbx = []
    for g in range(SSM_GROUPS):
        bg = bm[:, g * SSM_STATE:(g + 1) * SSM_STATE]
        cg = cm[:, g * SSM_STATE:(g + 1) * SSM_STATE]
        prod = jnp.concatenate([cg * bg[s:s + 1, :] for s in range(seq)], axis=0)
        hi = prod.astype(BF16)
        lo = (prod - hi.astype(F32)).astype(BF16)
        cbx.append(_dot(hi, ones) + _dot(lo, ones))
    row = lax.broadcasted_iota(jnp.int32, (seq, D_INNER), 0)
    for s in range(seq):
        cb_s = jnp.concatenate([cbx[g][s * seq:(s + 1) * seq, :] for g in range(SSM_GROUPS)], axis=1)
        decay = jnp.where(row >= s, jnp.exp(acum - acum[s:s + 1, :]), 0.0)
        y = y + cb_s * decay * xd[s:s + 1, :]

    y = y + dskip_ref[...] * xs
    y_ref[...] = _gate_and_group_norm(y, z_ref[...], norm_ref[...]).astype(y_ref.dtype)

    xw_sc[0:seq, :] = xd * jnp.exp(alast - acum)
    bpad_sc[0:seq, :] = bm
    xw_t = xw_sc[...].T
    elast = jnp.exp(alast)
    for g in range(SSM_GROUPS):
        dh = _dot(xw_t[g * GROUP_CH:(g + 1) * GROUP_CH, :].astype(BF16),
                  bpad_sc[:, g * SSM_STATE:(g + 1) * SSM_STATE].astype(BF16))
        for hh in range(HEADS_PER_GROUP):
            h = g * HEADS_PER_GROUP + hh
            nssm_ref[0, 0, h] = (sprev_ref[0, 0, h] * elast[:, h * hp:h * hp + 1]
                                 + dh[hh * hp:(hh + 1) * hp, :])


def ssm_step(proj, state_conv, state_ssm, new_ssm_stack, layer, w, bs, seq):
    n = bs * seq
    vec = lambda width: pl.BlockSpec((1, width), lambda b: (0, 0))
    state_block = (1, 1, SSM_HEADS, SSM_HEAD_DIM, SSM_STATE)
    in_specs = [
        pl.BlockSpec((seq, D_INNER), lambda b: (b, C_Z // D_INNER)),
        pl.BlockSpec((seq, D_INNER), lambda b: (b, C_XS // D_INNER)),
        pl.BlockSpec((seq, 1024), lambda b: (b, C_BC // 1024)),
        pl.BlockSpec((seq, D_INNER), lambda b: (b, C_DTX // D_INNER)),
        pl.BlockSpec((1, 1, D_CONV - 1, CONV_DIM), lambda b: (layer, b, 0, 0)),
        pl.BlockSpec(state_block, lambda b: (layer, b, 0, 0, 0)),
        pl.BlockSpec((D_CONV, CONV_DIM), lambda b: (0, 0)),
        vec(CONV_DIM), vec(D_INNER), vec(D_INNER), vec(D_INNER), vec(D_INNER),
    ]
    args = [proj, proj, proj, proj, state_conv, state_ssm, w["conv_w"], w["conv_b"], w["dt_bias_x"],
            w["a_neg_x"], w["d_skip_x"], w["ssm_norm"]]
    aliases = {}
    if new_ssm_stack is not None:
        in_specs.append(pl.BlockSpec(memory_space=pl.ANY))
        args.append(new_ssm_stack)
        aliases = {len(args) - 1: 2}
    return pl.pallas_call(
        _ssm_step_kernel,
        grid=(bs,),
        in_specs=in_specs,
        out_specs=[
            pl.BlockSpec((seq, D_INNER), lambda b: (b, 0)),
            pl.BlockSpec((1, D_CONV - 1, CONV_DIM), lambda b: (b, 0, 0)),
            pl.BlockSpec(state_block, lambda b: (layer, b, 0, 0, 0)),
        ],
        out_shape=[
            jax.ShapeDtypeStruct((n, D_INNER), F32),
            jax.ShapeDtypeStruct((bs, D_CONV - 1, CONV_DIM), F32),
            jax.ShapeDtypeStruct(state_ssm.shape, F32),
        ],
        scratch_shapes=[pltpu.VMEM((seq + 8, CONV_DIM), F32), pltpu.VMEM((LANES, D_INNER), F32),
                        pltpu.VMEM((LANES, SSM_GROUPS * SSM_STATE), F32)],
        input_output_aliases=aliases,
        compiler_params=_cparams(("arbitrary",)),
        name="ssm_step",
    )(*args)


def _rope(x, cos, sin_hi, sin_lo):
    return x * cos + pltpu.roll(x, 16, axis=1) * sin_hi + pltpu.roll(x, x.shape[1] - 16, axis=1) * sin_lo


def _kv_kernel(s_ref, g_ref, cos_ref, shi_ref, slo_ref, wk_ref, wv_ref,
               ckv_ref, kr_ref, *kv_out, with_kv):
    blk = s_ref[...]
    ckv = _rms(blk[:, :KV_LORA], g_ref[...])
    ckv_ref[...] = ckv
    tail = _rope(blk[:, KV_LORA:], cos_ref[...], shi_ref[...], slo_ref[...])
    kr_ref[...] = tail[:, S_ROPE - KV_LORA:S_ROPE - KV_LORA + QK_ROPE]
    if with_kv:
        kfull_ref, vt_ref = kv_out
        ckv16 = ckv.astype(BF16)
        lhs = jnp.concatenate([ckv16, tail.astype(BF16)], axis=1)
        kfull_ref[...] = _dot(lhs, wk_ref[...]).astype(kfull_ref.dtype)
        vt_ref[0] = _dot_nt(wv_ref[...], ckv16).astype(vt_ref.dtype)


def kv_project(small, w, tabs, with_kv):
    n = small.shape[0]
    tm = min(_pick(n, (512, 256, 128)), tabs[0].shape[0])
    ntab = tabs[0].shape[0] // tm
    tab = pl.BlockSpec((tm, LANES), lambda i: (i % ntab, 0))
    out_specs = [pl.BlockSpec((tm, KV_LORA), lambda i: (i, 0)), pl.BlockSpec((tm, QK_ROPE), lambda i: (i, 0))]
    out_shape = [jax.ShapeDtypeStruct((n, KV_LORA), F32), jax.ShapeDtypeStruct((n, QK_ROPE), F32)]
    if with_kv:
        out_specs += [pl.BlockSpec((tm, MLA_HEADS * HEAD_PAD), lambda i: (i, 0)),
                      pl.BlockSpec((1, MLA_HEADS * V_HEAD, tm), lambda i: (i, 0, 0))]
        out_shape += [jax.ShapeDtypeStruct((n, MLA_HEADS * HEAD_PAD), BF16),
                      jax.ShapeDtypeStruct((n // tm, MLA_HEADS * V_HEAD, tm), BF16)]
    return pl.pallas_call(
        functools.partial(_kv_kernel, with_kv=with_kv),
        grid=(n // tm,),
        in_specs=[
            pl.BlockSpec((tm, SMALL_DIM), lambda i: (i, 0)),
            pl.BlockSpec((1, KV_LORA), lambda i: (0, 0)),
            tab, tab, tab,
            pl.BlockSpec(w["wk_full"].shape, lambda i: (0, 0)),
            pl.BlockSpec(w["w_uv_t"].shape, lambda i: (0, 0)),
        ],
        out_specs=out_specs,
        out_shape=out_shape,
        compiler_params=_cparams(("parallel",)),
        name="kv_project",
    )(small, w["kv_norm"], *tabs, w["wk_full"], w["w_uv_t"])


def _q_kernel(x_ref, g_ref, cos_ref, shi_ref, slo_ref, wq_ref, *rest, absorb):
    u = _rms(x_ref[...].astype(F32), g_ref[...]).astype(BF16)
    q = _dot(u, wq_ref[...])
    tile = lambda r: jnp.concatenate([r[...]] * MLA_HEADS, axis=1)
    q = _rope(q, tile(cos_ref), tile(shi_ref), tile(slo_ref))
    if absorb:
        wabs_ref, q_ref, qa_ref = rest
        q_ref[...] = q.astype(q_ref.dtype)
        q16 = q.astype(BF16)
        for h in range(MLA_HEADS):
            qa_ref[:, h * KV_LORA:(h + 1) * KV_LORA] = _dot(
                q16[:, h * HEAD_PAD:(h + 1) * HEAD_PAD], wabs_ref[h]).astype(qa_ref.dtype)
    else:
        (q_ref,) = rest
        q_ref[...] = (q * SCORE_SCALE).astype(q_ref.dtype)


def q_project(proj, w, tabs, absorb):
    n = proj.shape[0]
    tm = min(_pick(n, (512, 256, 128)), tabs[0].shape[0])
    ntab = tabs[0].shape[0] // tm
    tab = pl.BlockSpec((tm, LANES), lambda i: (i % ntab, 0))
    qw = MLA_HEADS * HEAD_PAD
    in_specs = [
        pl.BlockSpec((tm, Q_LORA), lambda i: (i, C_Q // Q_LORA)),
        pl.BlockSpec((1, Q_LORA), lambda i: (0, 0)),
        tab, tab, tab,
        pl.BlockSpec((Q_LORA, qw), lambda i: (0, 0)),
    ]
    args = [proj, w["q_norm"], *tabs, w["w_qb_pad"]]
    if absorb:
        in_specs.append(pl.BlockSpec((MLA_HEADS, HEAD_PAD, KV_LORA), lambda i: (0, 0, 0)))
        args.append(w["w_abs"])
        out_specs = [pl.BlockSpec((tm, qw), lambda i: (i, 0)),
                     pl.BlockSpec((tm, MLA_HEADS * KV_LORA), lambda i: (i, 0))]
        out_shape = [jax.ShapeDtypeStruct((n, qw), F32),
                     jax.ShapeDtypeStruct((n, MLA_HEADS * KV_LORA), F32)]
    else:
        out_specs = pl.BlockSpec((tm, qw), lambda i: (i, 0))
        out_shape = jax.ShapeDtypeStruct((n, qw), BF16)
    return pl.pallas_call(
        functools.partial(_q_kernel, absorb=absorb),
        grid=(n // tm,),
        in_specs=in_specs,
        out_specs=out_specs,
        out_shape=out_shape,
        compiler_params=_cparams(("parallel",)),
        name="q_project",
    )(*args)


def _bcast_lanes(x, width):
    return jnp.concatenate([x] * (width // LANES), axis=1)


def _prefill_attn_kernel(q_ref, k_ref, vt_ref, o_ref, m_sc, l_sc, acc_sc):
    qi = pl.program_id(2)
    tq = q_ref.shape[0]
    tk = vt_ref.shape[2]
    nsplit, _, tc = acc_sc.shape
    head0 = lax.broadcasted_iota(jnp.int32, (2 * V_HEAD, tc), 0) < V_HEAD

    m_sc[...] = jnp.full_like(m_sc, NEG_BIG)
    l_sc[...] = jnp.zeros_like(l_sc)
    acc_sc[...] = jnp.zeros_like(acc_sc)

    def update(kis, masked):
        nkeys = [(c + 1) * tc if masked else tk for c in range(nsplit)]
        scores = []
        for ki in kis:
            start = pl.multiple_of(ki * tk, tk)
            for c in range(nsplit):
                for h in range(2):
                    qh = q_ref[c * tc:(c + 1) * tc, h * HEAD_PAD:(h + 1) * HEAD_PAD]
                    kh = k_ref[pl.ds(start, nkeys[c]), h * HEAD_PAD:(h + 1) * HEAD_PAD]
                    scores.append(_dot_nt(kh, qh))
        for n, ki in enumerate(kis):
            for c in range(nsplit):
                vt = vt_ref[ki, :, 0:nkeys[c]]
                alphas, pvs = [], []
                for h in range(2):
                    st = c * 2 + h
                    s = scores[n * 2 * nsplit + st]
                    if masked:
                        keep = (lax.broadcasted_iota(jnp.int32, (nkeys[c], tc), 0)
                                <= lax.broadcasted_iota(jnp.int32, (nkeys[c], tc), 1) + c * tc)
                        s = jnp.where(keep, s, NEG_BIG)
                    m_prev = m_sc[st]
                    m_next = jnp.maximum(m_prev, jnp.max(s, axis=0, keepdims=True))
                    p = jnp.exp2(s - m_next)
                    alpha = jnp.exp2(m_prev - m_next)
                    l_sc[st] = alpha * l_sc[st] + jnp.sum(p, axis=0, keepdims=True)
                    m_sc[st] = m_next
                    alphas.append(alpha)
                    pvs.append(_dot(vt, p.astype(BF16)))
                acc_sc[c] = (jnp.where(head0, alphas[0], alphas[1]) * acc_sc[c]
                             + jnp.where(head0, pvs[0], pvs[1]))

    def body(j, carry):
        update((2 * j, 2 * j + 1), False)
        return carry

    lax.fori_loop(0, qi // 2, body, 0)

    @pl.when(qi % 2 == 1)
    def _():
        update((qi - 1,), False)

    update((qi,), True)
    for c in range(nsplit):
        o_t = acc_sc[c] / jnp.where(head0, l_sc[2 * c], l_sc[2 * c + 1])
        o_ref[c * tc:(c + 1) * tc, :] = o_t.T.astype(o_ref.dtype)


def prefill_attention(q, kfull, vt, bs, seq):
    tk = vt.shape[2]
    tq = tk
    nq = seq // tq
    n = bs * seq
    nsplit = 2 if tq % (2 * LANES) == 0 else 1
    tc = tq // nsplit
    return pl.pallas_call(
        _prefill_attn_kernel,
        grid=(bs, MLA_HEADS // 2, nq),
        in_specs=[
            pl.BlockSpec((tq, 2 * HEAD_PAD), lambda b, hp, qi: (b * nq + qi, hp)),
            pl.BlockSpec((seq, 2 * HEAD_PAD), lambda b, hp, qi: (b, hp)),
            pl.BlockSpec((nq, 2 * V_HEAD, tk), lambda b, hp, qi: (b, hp, 0)),
        ],
        out_specs=pl.BlockSpec((tq, 2 * V_HEAD), lambda b, hp, qi: (b * nq + qi, hp)),
        out_shape=jax.ShapeDtypeStruct((n, MLA_HEADS * V_HEAD), BF16),
        scratch_shapes=[pltpu.VMEM((2 * nsplit, 1, tc), F32), pltpu.VMEM((2 * nsplit, 1, tc), F32),
                        pltpu.VMEM((nsplit, 2 * V_HEAD, tc), F32)],
        compiler_params=_cparams(("parallel", "parallel", "arbitrary")),
        name="prefill_attention",
    )(q, kfull, vt)


def _decode_attn_kernel(pt_ref, qa_ref, q_ref, cnew_ref, rnew_ref, kv_hbm, kr_hbm, o_ref,
                        kv_buf, kr_buf, kv_sem, kr_sem, *, layer, n_pages):
    b = pl.program_id(0)
    nb = pl.num_programs(0)
    seq = qa_ref.shape[0]
    rows = MLA_HEADS * seq
    page = kv_hbm.shape[2]
    slot = b % 2

    def page_copies(seq_idx, dst_slot, i):
        pid = pt_ref[seq_idx, i]
        return (pltpu.make_async_copy(kv_hbm.at[layer, pid],
                                      kv_buf.at[dst_slot, pl.ds(i * page, page), :],
                                      kv_sem.at[dst_slot]),
                pltpu.make_async_copy(kr_hbm.at[layer, pid],
                                      kr_buf.at[dst_slot, :, pl.ds(i * page, page)],
                                      kr_sem.at[dst_slot]))

    def start_fetch(seq_idx, dst_slot):
        for i in range(n_pages):
            for cp in page_copies(seq_idx, dst_slot, i):
                cp.start()

    @pl.when(b == 0)
    def _():
        start_fetch(0, 0)

    @pl.when(b + 1 < nb)
    def _():
        start_fetch(b + 1, 1 - slot)

    qa = jnp.concatenate([qa_ref[:, h * KV_LORA:(h + 1) * KV_LORA] for h in range(MLA_HEADS)],
                         axis=0) * SCORE_SCALE
    qr = jnp.concatenate([q_ref[:, h * HEAD_PAD:(h + 1) * HEAD_PAD] for h in range(MLA_HEADS)],
                         axis=0)[:, QK_NOPE:QK_NOPE + QK_ROPE] * SCORE_SCALE
    qa = qa.astype(BF16)
    qr = qr.astype(BF16)

    for i in range(n_pages):
        for cp in page_copies(b, slot, i):
            cp.wait()

    cp_pages = min(n_pages, DECODE_CHUNK_PAGES)
    ck = cp_pages * page
    chunks, scores = [], []
    for st in range(0, n_pages * page, ck):
        c = kv_buf[slot, pl.ds(st, ck), :].astype(BF16)
        r = kr_buf[slot, :, pl.ds(st, ck)].astype(BF16)
        chunks.append(c)
        scores.append(_dot_nt(qa, c) + _dot(qr, r))
    pad = page - seq
    c_new = jnp.concatenate([cnew_ref[...], jnp.zeros((pad, KV_LORA), F32)], axis=0).astype(BF16)
    r_new = jnp.concatenate([rnew_ref[...], jnp.zeros((pad, QK_ROPE), F32)], axis=0).astype(BF16)
    s_new = _dot_nt(qa, c_new) + _dot_nt(qr, r_new)
    tok = lax.broadcasted_iota(jnp.int32, (rows, page), 0) % seq
    key = lax.broadcasted_iota(jnp.int32, (rows, page), 1)
    chunks.append(c_new)
    scores.append(jnp.where(key <= tok, s_new, NEG_BIG))

    m = jnp.max(scores[0], axis=1, keepdims=True)
    for s in scores[1:]:
        m = jnp.maximum(m, jnp.max(s, axis=1, keepdims=True))
    l = jnp.zeros((rows, 1), F32)
    half = rows // 2
    acc_a = jnp.zeros((half, KV_LORA), F32)
    acc_b = jnp.zeros((rows - half, KV_LORA), F32)
    for c, s in zip(chunks, scores):
        p = jnp.exp2(s - m)
        l = l + jnp.sum(p, axis=1, keepdims=True)
        p16 = p.astype(BF16)
        acc_a = acc_a + _dot(p16[:half], c)
        acc_b = acc_b + _dot(p16[half:], c)
    o = jnp.concatenate([acc_a, acc_b], axis=0) / l
    for h in range(MLA_HEADS):
        o_ref[:, h * KV_LORA:(h + 1) * KV_LORA] = o[h * seq:(h + 1) * seq, :].astype(o_ref.dtype)


def decode_attention(qa, q, ckv, kr, cache_kv, cache_kr_t, page_table, layer, bs, seq):
    n_pages = page_table.shape[1]
    page = cache_kv.shape[2]
    n = bs * seq
    grid_spec = pltpu.PrefetchScalarGridSpec(
        num_scalar_prefetch=1,
        grid=(bs,),
        in_specs=[
            pl.BlockSpec((seq, MLA_HEADS * KV_LORA), lambda b, pt: (b, 0)),
            pl.BlockSpec((seq, MLA_HEADS * HEAD_PAD), lambda b, pt: (b, 0)),
            pl.BlockSpec((seq, KV_LORA), lambda b, pt: (b, 0)),
            pl.BlockSpec((seq, QK_ROPE), lambda b, pt: (b, 0)),
            pl.BlockSpec(memory_space=pl.ANY),
            pl.BlockSpec(memory_space=pl.ANY),
        ],
        out_specs=pl.BlockSpec((seq, MLA_HEADS * KV_LORA), lambda b, pt: (b, 0)),
        scratch_shapes=[pltpu.VMEM((2, n_pages * page, KV_LORA), F32),
                        pltpu.VMEM((2, QK_ROPE, n_pages * page), F32),
                        pltpu.SemaphoreType.DMA((2,)), pltpu.SemaphoreType.DMA((2,))],
    )
    return pl.pallas_call(
        functools.partial(_decode_attn_kernel, layer=layer, n_pages=n_pages),
        grid_spec=grid_spec,
        out_shape=jax.ShapeDtypeStruct((n, MLA_HEADS * KV_LORA), F32),
        compiler_params=_cparams(("arbitrary",)),
        name="decode_attention",
    )(page_table, qa, q, ckv, kr, cache_kv, cache_kr_t)


def _merge_kernel(x_ref, ys_ref, ym_ref, gs_ref, gm_ref, ws_ref, wm_ref, wo_ref, o_ref):
    ms = _dot(ys_ref[...].astype(BF16), ws_ref[...])
    mm = _dot(ym_ref[...].astype(BF16), wm_ref[...])
    merged = jax.nn.sigmoid(gs_ref[...].astype(F32)) * ms + jax.nn.sigmoid(gm_ref[...].astype(F32)) * mm
    o_ref[...] = x_ref[...] + _dot(merged.astype(BF16), wo_ref[...])


def merge(x, y_ssm, y_mla, proj, w, layer):
    n, d = x.shape
    tm = _pick(n, (512, 256, 128))
    full = lambda a: _resident_layer(a.shape, layer)
    return pl.pallas_call(
        _merge_kernel,
        grid=(n // tm,),
        in_specs=[
            pl.BlockSpec((tm, d), lambda i: (i, 0)),
            pl.BlockSpec((tm, D_INNER), lambda i: (i, 0)),
            pl.BlockSpec((tm, MLA_HEADS * V_HEAD), lambda i: (i, 0)),
            pl.BlockSpec((tm, d), lambda i: (i, C_GS // D_MODEL)),
            pl.BlockSpec((tm, d), lambda i: (i, C_GM // D_MODEL)),
            full(w["w_br_ssm"]), full(w["w_br_mla"]), full(w["w_out"]),
        ],
        out_specs=pl.BlockSpec((tm, d), lambda i: (i, 0)),
        out_shape=jax.ShapeDtypeStruct((n, d), F32),
        compiler_params=_cparams(("parallel",)),
        name="merge",
    )(x, y_ssm, y_mla, proj, proj, w["w_br_ssm"], w["w_br_mla"], w["w_out"])


def _ple_kernel(x_ref, p_ref, g_ref, wg_ref, wp_ref, fg_ref, o_ref, *, final):
    x = x_ref[...]
    gate = jax.nn.sigmoid(_dot(_rms(x, g_ref[...]).astype(BF16), wg_ref[...]))
    x = x + gate * _dot(p_ref[...].astype(BF16), wp_ref[...])
    if final:
        x = _rms(x, fg_ref[...])
    o_ref[...] = x


def ple(x, p, w, big_w, layer, final_norm, final):
    n, d = x.shape
    tm = _pick(n, (512, 256, 128))
    full = lambda a: _resident_layer(a.shape, layer)
    return pl.pallas_call(
        functools.partial(_ple_kernel, final=final),
        grid=(n // tm,),
        in_specs=[
            pl.BlockSpec((tm, d), lambda i: (i, 0)),
            pl.BlockSpec((tm, PLE_DIM), lambda i: (i, 0)),
            pl.BlockSpec((1, d), lambda i: (0, 0)),
            full(big_w["w_ple_gate"]), full(big_w["w_ple_proj"]),
            pl.BlockSpec((1, d), lambda i: (0, 0)),
        ],
        out_specs=pl.BlockSpec((tm, d), lambda i: (i, 0)),
        out_shape=jax.ShapeDtypeStruct((n, d), F32),
        compiler_params=_cparams(("parallel",)),
        name="ple",
    )(x, p, w["ple_norm"], big_w["w_ple_gate"], big_w["w_ple_proj"], final_norm)


def _rope_tables(pos, rows):
    half = QK_ROPE // 2
    inv = ROPE_BASE ** (-jnp.arange(half, dtype=F32) / half)
    ang = pos.astype(F32)[:, None] * inv[None, :]
    cos, sin = jnp.cos(ang), jnp.sin(ang)
    z = lambda w_: jnp.zeros((pos.shape[0], w_), F32)
    cos_t = jnp.concatenate([jnp.ones((pos.shape[0], QK_NOPE), F32), cos, cos, z(32)], axis=1)
    sin_hi = jnp.concatenate([z(QK_NOPE + half), sin, z(32)], axis=1)
    sin_lo = jnp.concatenate([z(QK_NOPE), -sin, z(half + 32)], axis=1)
    reps = max(1, rows // pos.shape[0])
    return tuple(jnp.tile(t, (reps, 1)) for t in (cos_t, sin_hi, sin_lo))


def _layer_weights(i, p):
    row = lambda v: v.reshape(1, -1).astype(F32)
    w_in = p["w_in"][i]
    off_z, off_xbc, off_dt, off_q, off_kv = 2048, 5120, 5152, 5664, 5952
    z = w_in[:, :off_z]
    xbc = w_in[:, off_z:off_xbc]
    dt = w_in[:, off_xbc:off_dt]
    q_lat = w_in[:, off_dt:off_q]
    kv_lat = w_in[:, off_q:off_q + KV_LORA]
    k_rope = w_in[:, off_q + KV_LORA:off_kv]
    gates = w_in[:, off_kv:]
    w_main = jnp.concatenate([z, xbc, gates, q_lat], axis=1).astype(BF16)
    w_main_x = jnp.concatenate([w_main, jnp.zeros((D_MODEL, C_DTX - MAIN_DIM), BF16),
                                jnp.repeat(dt, SSM_HEAD_DIM, axis=1).astype(BF16)], axis=1)
    zeros32 = jnp.zeros((D_MODEL, 32), F32)
    small = jnp.concatenate([kv_lat, dt, zeros32, k_rope, zeros32], axis=1)

    lane_pad = lambda v: jnp.pad(v.astype(F32), (0, LANES - v.shape[0])).reshape(1, LANES)
    a_neg = -jnp.exp(p["a_log"][i].astype(F32))

    w_qb = p["w_qb"][i].reshape(Q_LORA, MLA_HEADS, QK_NOPE + QK_ROPE)
    w_qb_pad = jnp.pad(w_qb, ((0, 0), (0, 0), (0, HEAD_PAD - QK_NOPE - QK_ROPE)))
    w_uk = p["w_uk"][i]
    wk_nope = jnp.pad(w_uk, ((0, 0), (0, 0), (0, HEAD_PAD - QK_NOPE))).reshape(KV_LORA, -1)
    place = jnp.zeros((LANES, MLA_HEADS, HEAD_PAD), F32)
    j = jnp.arange(QK_ROPE)
    place = place.at[S_ROPE - KV_LORA + j, :, QK_NOPE + j].set(1.0).reshape(LANES, -1)
    w_abs = jnp.pad(jnp.transpose(w_uk, (1, 2, 0)), ((0, 0), (0, HEAD_PAD - QK_NOPE), (0, 0)))
    w_uv = p["w_uv"][i]
    eye = jnp.eye(MLA_HEADS, dtype=F32)
    w_uv_bd = (jnp.transpose(w_uv, (1, 0, 2))[:, :, None, :] * eye[:, None, :, None]).reshape(
        MLA_HEADS * KV_LORA, MLA_HEADS * V_HEAD)

    return {
        "ffn1_norm": row(p["ffn1_norm"][i]), "ffn2_norm": row(p["ffn2_norm"][i]),
        "mix_norm": row(p["mix_norm"][i]),
        "w_main": w_main, "w_main_x": w_main_x,
        "w_small": small.astype(BF16),
        "conv_w": p["conv_w"][i].astype(F32), "conv_b": row(p["conv_b"][i]),
        "dt_bias_l": lane_pad(p["dt_bias"][i]), "a_neg_l": lane_pad(a_neg * math.log2(math.e)),
        "dt_bias_x": row(jnp.repeat(p["dt_bias"][i], SSM_HEAD_DIM)),
        "a_neg_x": row(jnp.repeat(a_neg, SSM_HEAD_DIM)),
        "d_skip_x": row(jnp.repeat(p["d_skip"][i], SSM_HEAD_DIM)),
        "ssm_norm": row(p["ssm_norm"][i]),
        "q_norm": row(p["q_norm"][i]), "kv_norm": row(p["kv_norm"][i]),
        "w_qb_pad": w_qb_pad.reshape(Q_LORA, -1).astype(BF16),
        "wk_full": jnp.concatenate([wk_nope, place], axis=0).astype(BF16),
        "w_uv_t": w_uv.reshape(KV_LORA, -1).T.astype(BF16),
        "w_abs": w_abs.astype(BF16),
        "w_uv_bd": w_uv_bd.astype(BF16),
        "ple_norm": row(p["ple_norm"][i]),
    }


def _trunk_layer(x, p_l, tabs, conv_prev, ssm_prev, w, ffn_w, layer, bs, seq, final_norm, final, paged):
    x = ffn_half_step(x, w["ffn1_norm"], ffn_w["ffn1_w_gu"], ffn_w["ffn1_w_down"], layer)
    if paged is None:
        proj, small = in_project(x, w["mix_norm"], w["w_main"], w["w_small"], BF16, (512, 256, 128))
    else:
        proj, small = in_project(x, w["mix_norm"], w["w_main_x"], w["w_small"], F32, (256, 128))
    if paged is None:
        y_ssm, new_conv, new_ssm = ssm_prefill(proj, small, conv_prev, ssm_prev, w, bs, seq)
        c_kv, k_rope, kfull, vt = kv_project(small, w, tabs, True)
        q = q_project(proj, w, tabs, False)
        y_mla = prefill_attention(q, kfull, vt, bs, seq)
    else:
        cache_kv, cache_kr_t, page_table, new_ssm_stack = paged
        y_ssm, new_conv, new_ssm = ssm_step(proj, conv_prev, ssm_prev, new_ssm_stack, layer, w, bs, seq)
        c_kv, k_rope = kv_project(small, w, tabs, False)
        q, qa = q_project(proj, w, tabs, True)
        o_lat = decode_attention(qa, q, c_kv, k_rope, cache_kv, cache_kr_t, page_table, layer, bs, seq)
        y_mla = norm_matmul(o_lat, None, w["w_uv_bd"], BF16, norm=False, tm_prefs=(256, 128))
    x = merge(x, y_ssm, y_mla, proj, ffn_w, layer)
    x = ffn_half_step(x, w["ffn2_norm"], ffn_w["ffn2_w_gu"], ffn_w["ffn2_w_down"], layer)
    x = ple(x, p_l, w, ffn_w, layer, final_norm, final)
    return x, c_kv, k_rope, new_conv, new_ssm


def kernel(x_prompt, x_sample, cache_kv, cache_kr, state_conv, state_ssm, page_table, p_prompt, p_sample,
           ffn1_norm, ffn1_w_gu, ffn1_w_down, mix_norm, w_in, conv_w, conv_b, dt_bias, a_log, d_skip,
           ssm_norm, q_norm, w_qb, kv_norm, w_uk, w_uv, w_br_ssm, w_br_mla, w_out,
           ffn2_norm, ffn2_w_gu, ffn2_w_down, ple_norm, w_ple_gate, w_ple_proj, final_norm):
    params = dict(ffn1_norm=ffn1_norm, ffn1_w_gu=ffn1_w_gu, ffn1_w_down=ffn1_w_down, mix_norm=mix_norm,
                  w_in=w_in, conv_w=conv_w, conv_b=conv_b, dt_bias=dt_bias, a_log=a_log, d_skip=d_skip,
                  ssm_norm=ssm_norm, q_norm=q_norm, w_qb=w_qb, kv_norm=kv_norm, w_uk=w_uk, w_uv=w_uv,
                  w_br_ssm=w_br_ssm, w_br_mla=w_br_mla, w_out=w_out, ffn2_norm=ffn2_norm,
                  ffn2_w_gu=ffn2_w_gu, ffn2_w_down=ffn2_w_down, ple_norm=ple_norm,
                  w_ple_gate=w_ple_gate, w_ple_proj=w_ple_proj)
    depth = w_in.shape[0]
    bp, sp, d = x_prompt.shape
    bsm, ss, _ = x_sample.shape
    past_len = page_table.shape[1] * cache_kv.shape[2]
    np_, ns = bp * sp, bsm * ss

    tabs_p = _rope_tables(jnp.arange(sp), sp)
    tabs_s = _rope_tables(past_len + jnp.arange(ss), _pick(ns, (256, 128)))
    conv0 = jnp.zeros((bp, D_CONV - 1, CONV_DIM), F32)
    ssm0 = jnp.zeros((bp, SSM_HEADS, SSM_HEAD_DIM, SSM_STATE), F32)
    fnorm = final_norm.reshape(1, -1).astype(F32)

    cache_kr_t = jnp.swapaxes(cache_kr, 2, 3)
    ffn_w = {name: cast_bf16(params[name])
             for name in ("ffn1_w_gu", "ffn1_w_down", "ffn2_w_gu", "ffn2_w_down", "w_br_ssm",
                          "w_br_mla", "w_out", "w_ple_gate", "w_ple_proj")}

    hp = x_prompt.reshape(np_, d)
    hs = x_sample.reshape(ns, d)
    outs_p, outs_s = [], []
    new_ssm_stack = None
    for i in range(depth):
        last = i == depth - 1
        w = _layer_weights(i, params)
        hs, *rest_s = _trunk_layer(hs, p_sample[i].reshape(ns, PLE_DIM), tabs_s, state_conv,
                                   state_ssm, w, ffn_w, i, bsm, ss, fnorm, last,
                                   (cache_kv, cache_kr_t, page_table, new_ssm_stack))
        hp, *rest_p = _trunk_layer(hp, p_prompt[i].reshape(np_, PLE_DIM), tabs_p, conv0, ssm0, w,
                                   ffn_w, i, bp, sp, fnorm, last, None)
        new_ssm_stack = rest_s[3]
        outs_p.append(rest_p)
        outs_s.append(rest_s)

    stack = lambda outs, k, shape: jnp.stack([o[k] for o in outs]).reshape((depth,) + shape)
    return (
        hp.reshape(bp, sp, d), hs.reshape(bsm, ss, d),
        stack(outs_p, 0, (bp, sp, KV_LORA)), stack(outs_p, 1, (bp, sp, QK_ROPE)),
        stack(outs_p, 2, (bp, D_CONV - 1, CONV_DIM)),
        stack(outs_p, 3, (bp, SSM_HEADS, SSM_HEAD_DIM, SSM_STATE)),
        stack(outs_s, 0, (bsm, ss, KV_LORA)), stack(outs_s, 1, (bsm, ss, QK_ROPE)),
        stack(outs_s, 2, (bsm, D_CONV - 1, CONV_DIM)),
        new_ssm_stack,
    )
```

```python
import functools
import math

import jax
import jax.numpy as jnp
from jax import lax
from jax.experimental import pallas as pl
from jax.experimental.pallas import tpu as pltpu

F32 = jnp.float32
BF16 = jnp.bfloat16

D_MODEL = 1024
D_FF = 2816
D_INNER = 2048
SSM_HEAD_DIM = 64
SSM_HEADS = 32
SSM_GROUPS = 4
HEADS_PER_GROUP = SSM_HEADS // SSM_GROUPS
SSM_STATE = 128
GROUP_CH = D_INNER // SSM_GROUPS
D_CONV = 4
CONV_DIM = D_INNER + 2 * SSM_GROUPS * SSM_STATE
SSD_CHUNK = 128
MLA_HEADS = 16
Q_LORA = 512
KV_LORA = 256
QK_NOPE = 64
QK_ROPE = 32
V_HEAD = 64
ROPE_BASE = 10000.0
ATTN_SCALE = (QK_NOPE + QK_ROPE) ** -0.5
SCORE_SCALE = ATTN_SCALE * math.log2(math.e)
PLE_DIM = 256
EPS = 1e-6
LANES = 128
HEAD_PAD = 128
DECODE_CHUNK_PAGES = 4
NEG_BIG = -1e30

C_Z, C_XS, C_BC, C_GS, C_GM, C_Q = 0, 2048, 4096, 5120, 6144, 7168
MAIN_DIM = 7680
C_DTX = 8192
MAIN_DIM_X = C_DTX + D_INNER
SMALL_DIM = 384
S_DT, S_ROPE = 256, 320

VMEM_LIMIT = 56 * 1024 * 1024


def _cparams(sem):
    return pltpu.CompilerParams(dimension_semantics=sem, vmem_limit_bytes=VMEM_LIMIT)


def _pick(n, prefs):
    for p in prefs:
        if n % p == 0:
            return p
    return n


def _dot(a, b):
    return jnp.dot(a, b, preferred_element_type=F32)


def _dot_nt(a, b):
    return lax.dot_general(a, b, (((1,), (1,)), ((), ())), preferred_element_type=F32)


def _silu(x):
    return x * jax.nn.sigmoid(x)


def _softplus(x):
    return jnp.maximum(x, 0.0) + jnp.log1p(jnp.exp(-jnp.abs(x)))


def _rms(x, g):
    return x * lax.rsqrt(jnp.mean(x * x, axis=-1, keepdims=True) + EPS) * g


def _nm_kernel(x_ref, g_ref, w_ref, o_ref, u_sc, *, norm):
    @pl.when(pl.program_id(1) == 0)
    def _():
        x = x_ref[...].astype(F32)
        if norm:
            x = _rms(x, g_ref[...])
        u_sc[...] = x.astype(BF16)

    o_ref[...] = _dot(u_sc[...], w_ref[...]).astype(o_ref.dtype)


def norm_matmul(x, g, w, out_dtype, *, norm=True, col_block=0, tm_prefs=(1024, 512, 256, 128)):
    n = x.shape[0]
    k, nout = w.shape
    tm = _pick(n, tm_prefs)
    tn = _pick(nout, (1280, 1024, 512, 384, 256, 128))
    if not norm:
        g = jnp.ones((1, k), F32)
    return pl.pallas_call(
        functools.partial(_nm_kernel, norm=norm),
        grid=(n // tm, nout // tn),
        in_specs=[
            pl.BlockSpec((tm, k), lambda i, j: (i, col_block)),
            pl.BlockSpec((1, k), lambda i, j: (0, 0)),
            pl.BlockSpec((k, tn), lambda i, j: (0, j)),
        ],
        out_specs=pl.BlockSpec((tm, tn), lambda i, j: (i, j)),
        out_shape=jax.ShapeDtypeStruct((n, nout), out_dtype),
        scratch_shapes=[pltpu.VMEM((tm, k), BF16)],
        compiler_params=_cparams(("parallel", "arbitrary")),
        name="norm_matmul",
    )(x, g, w)


def _resident(shape):
    return pl.BlockSpec(shape, lambda *_: (0,) * len(shape), pipeline_mode=pl.Buffered(1))


def _resident_layer(stacked_shape, layer):
    return pl.BlockSpec((None,) + tuple(stacked_shape[1:]),
                        lambda *_: (layer,) + (0,) * (len(stacked_shape) - 1),
                        pipeline_mode=pl.Buffered(1))


def _cast_kernel(x_ref, o_ref):
    o_ref[...] = x_ref[...].astype(o_ref.dtype)


def cast_bf16(x):
    depth, rows, cols = x.shape
    tr = _pick(rows, (256, 128))
    return pl.pallas_call(
        _cast_kernel,
        grid=(depth, rows // tr),
        in_specs=[pl.BlockSpec((1, tr, cols), lambda l, i: (l, i, 0))],
        out_specs=pl.BlockSpec((1, tr, cols), lambda l, i: (l, i, 0)),
        out_shape=jax.ShapeDtypeStruct(x.shape, BF16),
        compiler_params=_cparams(("parallel", "parallel")),
        name="cast_bf16",
    )(x)


def _in_proj_kernel(x_ref, g_ref, wm_ref, ws_ref, om_ref, os_ref, *, chunk):
    u = _rms(x_ref[...], g_ref[...]).astype(BF16)
    for c0 in range(0, wm_ref.shape[1], chunk):
        om_ref[:, c0:c0 + chunk] = _dot(u, wm_ref[:, c0:c0 + chunk]).astype(om_ref.dtype)
    os_ref[...] = _dot(u, ws_ref[...])


def in_project(x, g, w_main, w_small, main_dtype, tm_prefs):
    n, d = x.shape
    nmain = w_main.shape[1]
    tm = _pick(n, tm_prefs)
    return pl.pallas_call(
        functools.partial(_in_proj_kernel, chunk=_pick(nmain, (1280, 1024, 512))),
        grid=(n // tm,),
        in_specs=[
            pl.BlockSpec((tm, d), lambda i: (i, 0)),
            pl.BlockSpec((1, d), lambda i: (0, 0)),
            _resident(w_main.shape),
            _resident(w_small.shape),
        ],
        out_specs=[pl.BlockSpec((tm, nmain), lambda i: (i, 0)),
                   pl.BlockSpec((tm, SMALL_DIM), lambda i: (i, 0))],
        out_shape=[jax.ShapeDtypeStruct((n, nmain), main_dtype),
                   jax.ShapeDtypeStruct((n, SMALL_DIM), F32)],
        compiler_params=_cparams(("parallel",)),
        name="in_project",
    )(x, g, w_main, w_small)


def _ffn_kernel(x_ref, g_ref, wgu_ref, wd_ref, o_ref):
    x = x_ref[...]
    dff = wd_ref.shape[0]
    u = _rms(x, g_ref[...]).astype(BF16)
    a = _dot(u, wgu_ref[:, :dff])
    b = _dot(u, wgu_ref[:, dff:])
    h = (_silu(a) * b).astype(BF16)
    o_ref[...] = x + 0.5 * _dot(h, wd_ref[...])


def ffn_half_step(x, g, w_gu, w_down, layer):
    n, d = x.shape
    tm = _pick(n, (512, 256, 128))
    return pl.pallas_call(
        _ffn_kernel,
        grid=(n // tm,),
        in_specs=[
            pl.BlockSpec((tm, d), lambda i: (i, 0)),
            pl.BlockSpec((1, d), lambda i: (0, 0)),
            _resident_layer(w_gu.shape, layer),
            _resident_layer(w_down.shape, layer),
        ],
        out_specs=pl.BlockSpec((tm, d), lambda i: (i, 0)),
        out_shape=jax.ShapeDtypeStruct((n, d), F32),
        compiler_params=_cparams(("parallel",)),
        name="ffn_half_step",
    )(x, g, w_gu, w_down)


def _causal_conv(xpad_sc, rows, w_ref, b_ref):
    w = w_ref[...]
    win = xpad_sc[0:8 + rows, :]
    acc = win[8:, :] * w[D_CONV - 1:D_CONV, :]
    for k in range(D_CONV - 1):
        shifted = pltpu.roll(win, D_CONV - 1 - k, axis=0)[8:, :]
        acc = acc + shifted * w[k:k + 1, :]
    return _silu(acc + b_ref[...])


def _gate_and_group_norm(y, z, norm_w):
    y = y * _silu(z)
    parts = []
    for g in range(SSM_GROUPS):
        yg = y[:, g * GROUP_CH:(g + 1) * GROUP_CH]
        parts.append(yg * lax.rsqrt(jnp.mean(yg * yg, axis=-1, keepdims=True) + EPS))
    return jnp.concatenate(parts, axis=1) * norm_w


def _cumsum_rows(a):
    rows = a.shape[0]
    row = lax.broadcasted_iota(jnp.int32, a.shape, 0)
    s = 1
    while s < rows:
        a = a + jnp.where(row >= s, pltpu.roll(a, s, axis=0), 0.0)
        s *= 2
    return a


def _ssm_prefill_kernel(z_ref, xs_ref, bc_ref, dt_ref, cprev_ref, sprev_ref, convw_ref, convb_ref,
                        dtb_ref, aneg_ref, dskip_ref, norm_ref,
                        y_ref, nconv_ref, nssm_ref, xpad_sc, h_sc):
    c = pl.program_id(1)
    t = SSD_CHUNK
    hp = SSM_HEAD_DIM

    @pl.when(c == 0)
    def _():
        xpad_sc[0:8, :] = jnp.zeros((8, CONV_DIM), F32)
        xpad_sc[5:8, :] = cprev_ref[0]
        h_sc[...] = sprev_ref[0].reshape(D_INNER, SSM_STATE)

    xpad_sc[8:8 + t, 0:D_INNER] = xs_ref[...].astype(F32)
    xpad_sc[8:8 + t, D_INNER:CONV_DIM] = bc_ref[...].astype(F32)
    conv = _causal_conv(xpad_sc, t, convw_ref, convb_ref)
    tail = xpad_sc[8 + t - 3:8 + t, :]
    xpad_sc[5:8, :] = tail

    xs = conv[:, :D_INNER]
    gs = SSM_GROUPS * SSM_STATE
    bm = conv[:, D_INNER:D_INNER + gs]
    cm = conv[:, D_INNER + gs:]

    dt = _softplus(dt_ref[...] + dtb_ref[...])
    acum = _cumsum_rows(dt * aneg_ref[...])
    alast = acum[t - 1:t, :]
    acum_t = acum.T
    dt_t = dt.T
    wst_t = (dt * jnp.exp2(alast - acum)).T
    xs_t = xs.T

    def row_bcast(a, r, rows):
        one = jnp.broadcast_to(a[r:r + 1, :], (8, a.shape[1]))
        return jnp.concatenate([one] * (rows // 8), axis=0)

    tri = (lax.broadcasted_iota(jnp.int32, (t, t), 0) >= lax.broadcasted_iota(jnp.int32, (t, t), 1))
    low_half = lax.broadcasted_iota(jnp.int32, (t, 2 * hp), 1) < hp

    y_parts = []
    for g in range(SSM_GROUPS):
        bg = bm[:, g * SSM_STATE:(g + 1) * SSM_STATE]
        cg = cm[:, g * SSM_STATE:(g + 1) * SSM_STATE]
        cb = _dot_nt(cg.astype(BF16), bg.astype(BF16))
        for pair in range(HEADS_PER_GROUP // 2):
            h0 = g * HEADS_PER_GROUP + 2 * pair
            lhs = []
            for h in (h0, h0 + 1):
                colb = jnp.broadcast_to(acum[:, h:h + 1], (t, t))
                decay = jnp.where(tri, jnp.exp2(colb - row_bcast(acum_t, h, t)), 0.0)
                m = cb * decay * row_bcast(dt_t, h, t)
                ce = cg * jnp.exp2(colb)
                lhs.append(jnp.concatenate([m, ce], axis=1))
            lhs = jnp.concatenate(lhs, axis=0).astype(BF16)
            r0 = h0 * hp
            rhs_t = jnp.concatenate([xs_t[r0:r0 + 2 * hp, :], h_sc[r0:r0 + 2 * hp, :]],
                                    axis=1).astype(BF16)
            out = _dot_nt(lhs, rhs_t)
            y_parts.append(jnp.where(low_half, out[:t, :], out[t:, :]))
    y = jnp.concatenate(y_parts, axis=1) + dskip_ref[...] * xs
    y_ref[...] = _gate_and_group_norm(y, z_ref[...].astype(F32), norm_ref[...]).astype(y_ref.dtype)

    elast = jnp.exp2(alast)
    for g in range(SSM_GROUPS):
        bg = bm[:, g * SSM_STATE:(g + 1) * SSM_STATE].astype(BF16)
        xw = []
        for hh in range(HEADS_PER_GROUP):
            h = g * HEADS_PER_GROUP + hh
            xw.append(xs_t[h * hp:(h + 1) * hp, :] * row_bcast(wst_t, h, hp))
        dh = _dot(jnp.concatenate(xw, axis=0).astype(BF16), bg)
        for hh in range(HEADS_PER_GROUP):
            h = g * HEADS_PER_GROUP + hh
            rows = slice(h * hp, (h + 1) * hp)
            h_sc[rows, :] = h_sc[rows, :] * elast[:, h:h + 1] + dh[hh * hp:(hh + 1) * hp, :]

    @pl.when(c == pl.num_programs(1) - 1)
    def _():
        nconv_ref[0] = tail
        nssm_ref[0] = h_sc[...].reshape(SSM_HEADS, SSM_HEAD_DIM, SSM_STATE)


def ssm_prefill(proj, small, conv_prev, ssm_prev, w, bs, seq):
    t = SSD_CHUNK
    nc = seq // t
    n = bs * seq
    row = lambda b, c: b * nc + c
    vec = lambda width: pl.BlockSpec((1, width), lambda b, c: (0, 0))
    return pl.pallas_call(
        _ssm_prefill_kernel,
        grid=(bs, nc),
        in_specs=[
            pl.BlockSpec((t, D_INNER), lambda b, c: (row(b, c), C_Z // D_INNER)),
            pl.BlockSpec((t, D_INNER), lambda b, c: (row(b, c), C_XS // D_INNER)),
            pl.BlockSpec((t, 1024), lambda b, c: (row(b, c), C_BC // 1024)),
            pl.BlockSpec((t, LANES), lambda b, c: (row(b, c), S_DT // LANES)),
            pl.BlockSpec((1, D_CONV - 1, CONV_DIM), lambda b, c: (b, 0, 0)),
            pl.BlockSpec((1, SSM_HEADS, SSM_HEAD_DIM, SSM_STATE), lambda b, c: (b, 0, 0, 0)),
            pl.BlockSpec((D_CONV, CONV_DIM), lambda b, c: (0, 0)),
            vec(CONV_DIM), vec(LANES), vec(LANES), vec(D_INNER), vec(D_INNER),
        ],
        out_specs=[
            pl.BlockSpec((t, D_INNER), lambda b, c: (row(b, c), 0)),
            pl.BlockSpec((1, D_CONV - 1, CONV_DIM), lambda b, c: (b, 0, 0)),
            pl.BlockSpec((1, SSM_HEADS, SSM_HEAD_DIM, SSM_STATE), lambda b, c: (b, 0, 0, 0)),
        ],
        out_shape=[
            jax.ShapeDtypeStruct((n, D_INNER), F32),
            jax.ShapeDtypeStruct((bs, D_CONV - 1, CONV_DIM), F32),
            jax.ShapeDtypeStruct((bs, SSM_HEADS, SSM_HEAD_DIM, SSM_STATE), F32),
        ],
        scratch_shapes=[pltpu.VMEM((t + 8, CONV_DIM), F32), pltpu.VMEM((D_INNER, SSM_STATE), F32)],
        compiler_params=_cparams(("parallel", "arbitrary")),
        name="ssm_prefill",
    )(proj, proj, proj, small, conv_prev, ssm_prev, w["conv_w"], w["conv_b"], w["dt_bias_l"],
      w["a_neg_l"], w["d_skip_x"], w["ssm_norm"])


def _ssm_step_kernel(z_ref, xs_ref, bc_ref, dtx_ref, cprev_ref, sprev_ref, convw_ref, convb_ref,
                     dtbx_ref, anegx_ref, dskip_ref, norm_ref, *rest):
    y_ref, nconv_ref, nssm_ref, xpad_sc, xw_sc, bpad_sc = rest[-6:]
    seq = xs_ref.shape[0]
    hp = SSM_HEAD_DIM
    gs = SSM_GROUPS * SSM_STATE

    @pl.when(pl.program_id(0) == 0)
    def _():
        xw_sc[...] = jnp.zeros_like(xw_sc)
        bpad_sc[...] = jnp.zeros_like(bpad_sc)
        xpad_sc[0:8, :] = jnp.zeros((8, CONV_DIM), F32)

    xpad_sc[5:8, :] = cprev_ref[0, 0]
    xpad_sc[8:8 + seq, 0:D_INNER] = xs_ref[...]
    xpad_sc[8:8 + seq, D_INNER:CONV_DIM] = bc_ref[...]
    conv = _causal_conv(xpad_sc, seq, convw_ref, convb_ref)
    nconv_ref[0] = xpad_sc[8 + seq - 3:8 + seq, :]

    xs = conv[:, :D_INNER]
    bm = conv[:, D_INNER:D_INNER + gs]
    cm = conv[:, D_INNER + gs:]

    dtx = _softplus(dtx_ref[...] + dtbx_ref[...])
    acum = _cumsum_rows(dtx * anegx_ref[...])
    alast = acum[seq - 1:seq, :]
    xd = xs * dtx

    c_all = jnp.concatenate([cm[:, g * SSM_STATE:(g + 1) * SSM_STATE] for g in range(SSM_GROUPS)],
                            axis=0).astype(BF16)
    y_off = []
    for g in range(SSM_GROUPS):
        hg = sprev_ref[0, 0, g * HEADS_PER_GROUP:(g + 1) * HEADS_PER_GROUP].reshape(GROUP_CH, SSM_STATE)
        y_off.append(_dot_nt(c_all, hg.astype(BF16))[g * seq:(g + 1) * seq, :])
    y = jnp.concatenate(y_off, axis=1) * jnp.exp(acum)

    ones = jnp.ones((SSM_STATE, GROUP_CH), BF16)
    cbx = []
    for g in range(SSM_GROUPS):
        bg = bm[:, g * SSM_STATE:(g + 1) * SSM_STATE]
        cg = cm[:, g * SSM_STATE:(g + 1) * SSM_STATE]
        prod = jnp.concatenate([cg * bg[s:s + 1, :] for s in range(seq)], axis=0)
        hi = prod.astype(BF16)
        lo = (prod - hi.astype(F32)).astype(BF16)
        cbx.append(_dot(hi, ones) + _dot(lo, ones))
    row = lax.broadcasted_iota(jnp.int32, (seq, D_INNER), 0)
    for s in range(seq):
        cb_s = jnp.concatenate([cbx[g][s * seq:(s + 1) * seq, :] for g in range(SSM_GROUPS)], axis=1)
        decay = jnp.where(row >= s, jnp.exp(acum - acum[s:s + 1, :]), 0.0)
        y = y + cb_s * decay * xd[s:s + 1, :]

    y = y + dskip_ref[...] * xs
    y_ref[...] = _gate_and_group_norm(y, z_ref[...], norm_ref[...]).astype(y_ref.dtype)

    xw_sc[0:seq, :] = xd * jnp.exp(alast - acum)
    bpad_sc[0:seq, :] = bm
    xw_t = xw_sc[...].T
    elast = jnp.exp(alast)
    for g in range(SSM_GROUPS):
        dh = _dot(xw_t[g * GROUP_CH:(g + 1) * GROUP_CH, :].astype(BF16),
                  bpad_sc[:, g * SSM_STATE:(g + 1) * SSM_STATE].astype(BF16))
        for hh in range(HEADS_PER_GROUP):
            h = g * HEADS_PER_GROUP + hh
            nssm_ref[0, 0, h] = (sprev_ref[0, 0, h] * elast[:, h * hp:h * hp + 1]
                                 + dh[hh * hp:(hh + 1) * hp, :])


def ssm_step(proj, state_conv, state_ssm, new_ssm_stack, layer, w, bs, seq):
    n = bs * seq
    vec = lambda width: pl.BlockSpec((1, width), lambda b: (0, 0))
    state_block = (1, 1, SSM_HEADS, SSM_HEAD_DIM, SSM_STATE)
    in_specs = [
        pl.BlockSpec((seq, D_INNER), lambda b: (b, C_Z // D_INNER)),
        pl.BlockSpec((seq, D_INNER), lambda b: (b, C_XS // D_INNER)),
        pl.BlockSpec((seq, 1024), lambda b: (b, C_BC // 1024)),
        pl.BlockSpec((seq, D_INNER), lambda b: (b, C_DTX // D_INNER)),
        pl.BlockSpec((1, 1, D_CONV - 1, CONV_DIM), lambda b: (layer, b, 0, 0)),
        pl.BlockSpec(state_block, lambda b: (layer, b, 0, 0, 0)),
        pl.BlockSpec((D_CONV, CONV_DIM), lambda b: (0, 0)),
        vec(CONV_DIM), vec(D_INNER), vec(D_INNER), vec(D_INNER), vec(D_INNER),
    ]
    args = [proj, proj, proj, proj, state_conv, state_ssm, w["conv_w"], w["conv_b"], w["dt_bias_x"],
            w["a_neg_x"], w["d_skip_x"], w["ssm_norm"]]
    aliases = {}
    if new_ssm_stack is not None:
        in_specs.append(pl.BlockSpec(memory_space=pl.ANY))
        args.append(new_ssm_stack)
        aliases = {len(args) - 1: 2}
    return pl.pallas_call(
        _ssm_step_kernel,
        grid=(bs,),
        in_specs=in_specs,
        out_specs=[
            pl.BlockSpec((seq, D_INNER), lambda b: (b, 0)),
            pl.BlockSpec((1, D_CONV - 1, CONV_DIM), lambda b: (b, 0, 0)),
            pl.BlockSpec(state_block, lambda b: (layer, b, 0, 0, 0)),
        ],
        out_shape=[
            jax.ShapeDtypeStruct((n, D_INNER), F32),
            jax.ShapeDtypeStruct((bs, D_CONV - 1, CONV_DIM), F32),
            jax.ShapeDtypeStruct(state_ssm.shape, F32),
        ],
        scratch_shapes=[pltpu.VMEM((seq + 8, CONV_DIM), F32), pltpu.VMEM((LANES, D_INNER), F32),
                        pltpu.VMEM((LANES, SSM_GROUPS * SSM_STATE), F32)],
        input_output_aliases=aliases,
        compiler_params=_cparams(("arbitrary",)),
        name="ssm_step",
    )(*args)


def _rope(x, cos, sin_hi, sin_lo):
    return x * cos + pltpu.roll(x, 16, axis=1) * sin_hi + pltpu.roll(x, x.shape[1] - 16, axis=1) * sin_lo


def _kv_kernel(s_ref, g_ref, cos_ref, shi_ref, slo_ref, wk_ref, wv_ref,
               ckv_ref, kr_ref, *kv_out, with_kv):
    blk = s_ref[...]
    ckv = _rms(blk[:, :KV_LORA], g_ref[...])
    ckv_ref[...] = ckv
    tail = _rope(blk[:, KV_LORA:], cos_ref[...], shi_ref[...], slo_ref[...])
    kr_ref[...] = tail[:, S_ROPE - KV_LORA:S_ROPE - KV_LORA + QK_ROPE]
    if with_kv:
        kfull_ref, vt_ref = kv_out
        ckv16 = ckv.astype(BF16)
        lhs = jnp.concatenate([ckv16, tail.astype(BF16)], axis=1)
        kfull_ref[...] = _dot(lhs, wk_ref[...]).astype(kfull_ref.dtype)
        vt_ref[0] = _dot_nt(wv_ref[...], ckv16).astype(vt_ref.dtype)


def kv_project(small, w, tabs, with_kv):
    n = small.shape[0]
    tm = min(_pick(n, (512, 256, 128)), tabs[0].shape[0])
    ntab = tabs[0].shape[0] // tm
    tab = pl.BlockSpec((tm, LANES), lambda i: (i % ntab, 0))
    out_specs = [pl.BlockSpec((tm, KV_LORA), lambda i: (i, 0)), pl.BlockSpec((tm, QK_ROPE), lambda i: (i, 0))]
    out_shape = [jax.ShapeDtypeStruct((n, KV_LORA), F32), jax.ShapeDtypeStruct((n, QK_ROPE), F32)]
    if with_kv:
        out_specs += [pl.BlockSpec((tm, MLA_HEADS * HEAD_PAD), lambda i: (i, 0)),
                      pl.BlockSpec((1, MLA_HEADS * V_HEAD, tm), lambda i: (i, 0, 0))]
        out_shape += [jax.ShapeDtypeStruct((n, MLA_HEADS * HEAD_PAD), BF16),
                      jax.ShapeDtypeStruct((n // tm, MLA_HEADS * V_HEAD, tm), BF16)]
    return pl.pallas_call(
        functools.partial(_kv_kernel, with_kv=with_kv),
        grid=(n // tm,),
        in_specs=[
            pl.BlockSpec((tm, SMALL_DIM), lambda i: (i, 0)),
            pl.BlockSpec((1, KV_LORA), lambda i: (0, 0)),
            tab, tab, tab,
            pl.BlockSpec(w["wk_full"].shape, lambda i: (0, 0)),
            pl.BlockSpec(w["w_uv_t"].shape, lambda i: (0, 0)),
        ],
        out_specs=out_specs,
        out_shape=out_shape,
        compiler_params=_cparams(("parallel",)),
        name="kv_project",
    )(small, w["kv_norm"], *tabs, w["wk_full"], w["w_uv_t"])


def _q_kernel(x_ref, g_ref, cos_ref, shi_ref, slo_ref, wq_ref, *rest, absorb):
    u = _rms(x_ref[...].astype(F32), g_ref[...]).astype(BF16)
    q = _dot(u, wq_ref[...])
    tile = lambda r: jnp.concatenate([r[...]] * MLA_HEADS, axis=1)
    q = _rope(q, tile(cos_ref), tile(shi_ref), tile(slo_ref))
    if absorb:
        wabs_ref, q_ref, qa_ref = rest
        q_ref[...] = q.astype(q_ref.dtype)
        q16 = q.astype(BF16)
        for h in range(MLA_HEADS):
            qa_ref[:, h * KV_LORA:(h + 1) * KV_LORA] = _dot(
                q16[:, h * HEAD_PAD:(h + 1) * HEAD_PAD], wabs_ref[h]).astype(qa_ref.dtype)
    else:
        (q_ref,) = rest
        q_ref[...] = (q * SCORE_SCALE).astype(q_ref.dtype)


def q_project(proj, w, tabs, absorb):
    n = proj.shape[0]
    tm = min(_pick(n, (512, 256, 128)), tabs[0].shape[0])
    ntab = tabs[0].shape[0] // tm
    tab = pl.BlockSpec((tm, LANES), lambda i: (i % ntab, 0))
    qw = MLA_HEADS * HEAD_PAD
    in_specs = [
        pl.BlockSpec((tm, Q_LORA), lambda i: (i, C_Q // Q_LORA)),
        pl.BlockSpec((1, Q_LORA), lambda i: (0, 0)),
        tab, tab, tab,
        pl.BlockSpec((Q_LORA, qw), lambda i: (0, 0)),
    ]
    args = [proj, w["q_norm"], *tabs, w["w_qb_pad"]]
    if absorb:
        in_specs.append(pl.BlockSpec((MLA_HEADS, HEAD_PAD, KV_LORA), lambda i: (0, 0, 0)))
        args.append(w["w_abs"])
        out_specs = [pl.BlockSpec((tm, qw), lambda i: (i, 0)),
                     pl.BlockSpec((tm, MLA_HEADS * KV_LORA), lambda i: (i, 0))]
        out_shape = [jax.ShapeDtypeStruct((n, qw), F32),
                     jax.ShapeDtypeStruct((n, MLA_HEADS * KV_LORA), F32)]
    else:
        out_specs = pl.BlockSpec((tm, qw), lambda i: (i, 0))
        out_shape = jax.ShapeDtypeStruct((n, qw), BF16)
    return pl.pallas_call(
        functools.partial(_q_kernel, absorb=absorb),
        grid=(n // tm,),
        in_specs=in_specs,
        out_specs=out_specs,
        out_shape=out_shape,
        compiler_params=_cparams(("parallel",)),
        name="q_project",
    )(*args)


def _bcast_lanes(x, width):
    return jnp.concatenate([x] * (width // LANES), axis=1)


def _prefill_attn_kernel(q_ref, k_ref, vt_ref, o_ref, m_sc, l_sc, acc_sc):
    qi = pl.program_id(2)
    tq = q_ref.shape[0]
    tk = vt_ref.shape[2]
    nsplit, _, tc = acc_sc.shape
    head0 = lax.broadcasted_iota(jnp.int32, (2 * V_HEAD, tc), 0) < V_HEAD

    m_sc[...] = jnp.full_like(m_sc, NEG_BIG)
    l_sc[...] = jnp.zeros_like(l_sc)
    acc_sc[...] = jnp.zeros_like(acc_sc)

    def update(kis, masked):
        nkeys = [(c + 1) * tc if masked else tk for c in range(nsplit)]
        scores = []
        for ki in kis:
            start = pl.multiple_of(ki * tk, tk)
            for c in range(nsplit):
                for h in range(2):
                    qh = q_ref[c * tc:(c + 1) * tc, h * HEAD_PAD:(h + 1) * HEAD_PAD]
                    kh = k_ref[pl.ds(start, nkeys[c]), h * HEAD_PAD:(h + 1) * HEAD_PAD]
                    scores.append(_dot_nt(kh, qh))
        for n, ki in enumerate(kis):
            for c in range(nsplit):
                vt = vt_ref[ki, :, 0:nkeys[c]]
                alphas, pvs = [], []
                for h in range(2):
                    st = c * 2 + h
                    s = scores[n * 2 * nsplit + st]
                    if masked:
                        keep = (lax.broadcasted_iota(jnp.int32, (nkeys[c], tc), 0)
                                <= lax.broadcasted_iota(jnp.int32, (nkeys[c], tc), 1) + c * tc)
                        s = jnp.where(keep, s, NEG_BIG)
                    m_prev = m_sc[st]
                    m_next = jnp.maximum(m_prev, jnp.max(s, axis=0, keepdims=True))
                    p = jnp.exp2(s - m_next)
                    alpha = jnp.exp2(m_prev - m_next)
                    l_sc[st] = alpha * l_sc[st] + jnp.sum(p, axis=0, keepdims=True)
                    m_sc[st] = m_next
                    alphas.append(alpha)
                    pvs.append(_dot(vt, p.astype(BF16)))
                acc_sc[c] = (jnp.where(head0, alphas[0], alphas[1]) * acc_sc[c]
                             + jnp.where(head0, pvs[0], pvs[1]))

    def body(j, carry):
        update((2 * j, 2 * j + 1), False)
        return carry

    lax.fori_loop(0, qi // 2, body, 0)

    @pl.when(qi % 2 == 1)
    def _():
        update((qi - 1,), False)

    update((qi,), True)
    for c in range(nsplit):
        o_t = acc_sc[c] / jnp.where(head0, l_sc[2 * c], l_sc[2 * c + 1])
        o_ref[c * tc:(c + 1) * tc, :] = o_t.T.astype(o_ref.dtype)


def prefill_attention(q, kfull, vt, bs, seq):
    tk = vt.shape[2]
    tq = tk
    nq = seq // tq
    n = bs * seq
    nsplit = 2 if tq % (2 * LANES) == 0 else 1
    tc = tq // nsplit
    return pl.pallas_call(
        _prefill_attn_kernel,
        grid=(bs, MLA_HEADS // 2, nq),
        in_specs=[
            pl.BlockSpec((tq, 2 * HEAD_PAD), lambda b, hp, qi: (b * nq + qi, hp)),
            pl.BlockSpec((seq, 2 * HEAD_PAD), lambda b, hp, qi: (b, hp)),
            pl.BlockSpec((nq, 2 * V_HEAD, tk), lambda b, hp, qi: (b, hp, 0)),
        ],
        out_specs=pl.BlockSpec((tq, 2 * V_HEAD), lambda b, hp, qi: (b * nq + qi, hp)),
        out_shape=jax.ShapeDtypeStruct((n, MLA_HEADS * V_HEAD), BF16),
        scratch_shapes=[pltpu.VMEM((2 * nsplit, 1, tc), F32), pltpu.VMEM((2 * nsplit, 1, tc), F32),
                        pltpu.VMEM((nsplit, 2 * V_HEAD, tc), F32)],
        compiler_params=_cparams(("parallel", "parallel", "arbitrary")),
        name="prefill_attention",
    )(q, kfull, vt)


def _decode_attn_kernel(pt_ref, qa_ref, q_ref, cnew_ref, rnew_ref, kv_hbm, kr_hbm, o_ref,
                        kv_buf, kr_buf, kv_sem, kr_sem, *, layer, n_pages):
    b = pl.program_id(0)
    nb = pl.num_programs(0)
    seq = qa_ref.shape[0]
    rows = MLA_HEADS * seq
    page = kv_hbm.shape[2]
    slot = b % 2

    def page_copies(seq_idx, dst_slot, i):
        pid = pt_ref[seq_idx, i]
        return (pltpu.make_async_copy(kv_hbm.at[layer, pid],
                                      kv_buf.at[dst_slot, pl.ds(i * page, page), :],
                                      kv_sem.at[dst_slot]),
                pltpu.make_async_copy(kr_hbm.at[layer, pid],
                                      kr_buf.at[dst_slot, i],
                                      kr_sem.at[dst_slot]))

    def start_fetch(seq_idx, dst_slot):
        for i in range(n_pages):
            for cp in page_copies(seq_idx, dst_slot, i):
                cp.start()

    @pl.when(b == 0)
    def _():
        start_fetch(0, 0)

    @pl.when(b + 1 < nb)
    def _():
        start_fetch(b + 1, 1 - slot)

    qa = jnp.concatenate([qa_ref[:, h * KV_LORA:(h + 1) * KV_LORA] for h in range(MLA_HEADS)],
                         axis=0) * SCORE_SCALE
    qr = jnp.concatenate([q_ref[:, h * HEAD_PAD:(h + 1) * HEAD_PAD] for h in range(MLA_HEADS)],
                         axis=0)[:, QK_NOPE:QK_NOPE + QK_ROPE] * SCORE_SCALE
    qa = qa.astype(BF16)
    qr = qr.astype(BF16)

    for i in range(n_pages):
        for cp in page_copies(b, slot, i):
            cp.wait()

    cp_pages = min(n_pages, DECODE_CHUNK_PAGES)
    ck = cp_pages * page
    chunks, scores = [], []
    for p0 in range(0, n_pages, cp_pages):
        c = kv_buf[slot, pl.ds(p0 * page, ck), :].astype(BF16)
        r = jnp.concatenate([kr_buf[slot, i] for i in range(p0, p0 + cp_pages)], axis=1).astype(BF16)
        chunks.append(c)
        scores.append(_dot_nt(qa, c) + _dot(qr, r))
    pad = page - seq
    c_new = jnp.concatenate([cnew_ref[...], jnp.zeros((pad, KV_LORA), F32)], axis=0).astype(BF16)
    r_new = jnp.concatenate([rnew_ref[...], jnp.zeros((pad, QK_ROPE), F32)], axis=0).astype(BF16)
    s_new = _dot_nt(qa, c_new) + _dot_nt(qr, r_new)
    tok = lax.broadcasted_iota(jnp.int32, (rows, page), 0) % seq
    key = lax.broadcasted_iota(jnp.int32, (rows, page), 1)
    chunks.append(c_new)
    scores.append(jnp.where(key <= tok, s_new, NEG_BIG))

    m = jnp.max(scores[0], axis=1, keepdims=True)
    for s in scores[1:]:
        m = jnp.maximum(m, jnp.max(s, axis=1, keepdims=True))
    l = jnp.zeros((rows, 1), F32)
    half = rows // 2
    acc_a = jnp.zeros((half, KV_LORA), F32)
    acc_b = jnp.zeros((rows - half, KV_LORA), F32)
    for c, s in zip(chunks, scores):
        p = jnp.exp2(s - m)
        l = l + jnp.sum(p, axis=1, keepdims=True)
        p16 = p.astype(BF16)
        acc_a = acc_a + _dot(p16[:half], c)
        acc_b = acc_b + _dot(p16[half:], c)
    o = jnp.concatenate([acc_a, acc_b], axis=0) / l
    for h in range(MLA_HEADS):
        o_ref[:, h * KV_LORA:(h + 1) * KV_LORA] = o[h * seq:(h + 1) * seq, :].astype(o_ref.dtype)


def decode_attention(qa, q, ckv, kr, cache_kv, cache_kr_t, page_table, layer, bs, seq):
    n_pages = page_table.shape[1]
    page = cache_kv.shape[2]
    n = bs * seq
    grid_spec = pltpu.PrefetchScalarGridSpec(
        num_scalar_prefetch=1,
        grid=(bs,),
        in_specs=[
            pl.BlockSpec((seq, MLA_HEADS * KV_LORA), lambda b, pt: (b, 0)),
            pl.BlockSpec((seq, MLA_HEADS * HEAD_PAD), lambda b, pt: (b, 0)),
            pl.BlockSpec((seq, KV_LORA), lambda b, pt: (b, 0)),
            pl.BlockSpec((seq, QK_ROPE), lambda b, pt: (b, 0)),
            pl.BlockSpec(memory_space=pl.ANY),
            pl.BlockSpec(memory_space=pl.ANY),
        ],
        out_specs=pl.BlockSpec((seq, MLA_HEADS * KV_LORA), lambda b, pt: (b, 0)),
        scratch_shapes=[pltpu.VMEM((2, n_pages * page, KV_LORA), F32),
                        pltpu.VMEM((2, n_pages, QK_ROPE, page), F32),
                        pltpu.SemaphoreType.DMA((2,)), pltpu.SemaphoreType.DMA((2,))],
    )
    return pl.pallas_call(
        functools.partial(_decode_attn_kernel, layer=layer, n_pages=n_pages),
        grid_spec=grid_spec,
        out_shape=jax.ShapeDtypeStruct((n, MLA_HEADS * KV_LORA), F32),
        compiler_params=_cparams(("arbitrary",)),
        name="decode_attention",
    )(page_table, qa, q, ckv, kr, cache_kv, cache_kr_t)


def _merge_kernel(x_ref, ys_ref, ym_ref, gs_ref, gm_ref, ws_ref, wm_ref, wo_ref, o_ref):
    ms = _dot(ys_ref[...].astype(BF16), ws_ref[...])
    mm = _dot(ym_ref[...].astype(BF16), wm_ref[...])
    merged = jax.nn.sigmoid(gs_ref[...].astype(F32)) * ms + jax.nn.sigmoid(gm_ref[...].astype(F32)) * mm
    o_ref[...] = x_ref[...] + _dot(merged.astype(BF16), wo_ref[...])


def merge(x, y_ssm, y_mla, proj, w, layer):
    n, d = x.shape
    tm = _pick(n, (512, 256, 128))
    full = lambda a: _resident_layer(a.shape, layer)
    return pl.pallas_call(
        _merge_kernel,
        grid=(n // tm,),
        in_specs=[
            pl.BlockSpec((tm, d), lambda i: (i, 0)),
            pl.BlockSpec((tm, D_INNER), lambda i: (i, 0)),
            pl.BlockSpec((tm, MLA_HEADS * V_HEAD), lambda i: (i, 0)),
            pl.BlockSpec((tm, d), lambda i: (i, C_GS // D_MODEL)),
            pl.BlockSpec((tm, d), lambda i: (i, C_GM // D_MODEL)),
            full(w["w_br_ssm"]), full(w["w_br_mla"]), full(w["w_out"]),
        ],
        out_specs=pl.BlockSpec((tm, d), lambda i: (i, 0)),
        out_shape=jax.ShapeDtypeStruct((n, d), F32),
        compiler_params=_cparams(("parallel",)),
        name="merge",
    )(x, y_ssm, y_mla, proj, proj, w["w_br_ssm"], w["w_br_mla"], w["w_out"])


def _ple_kernel(x_ref, p_ref, g_ref, wg_ref, wp_ref, fg_ref, o_ref, *, final):
    x = x_ref[...]
    gate = jax.nn.sigmoid(_dot(_rms(x, g_ref[...]).astype(BF16), wg_ref[...]))
    x = x + gate * _dot(p_ref[...].astype(BF16), wp_ref[...])
    if final:
        x = _rms(x, fg_ref[...])
    o_ref[...] = x


def ple(x, p, w, big_w, layer, final_norm, final):
    n, d = x.shape
    tm = _pick(n, (512, 256, 128))
    full = lambda a: _resident_layer(a.shape, layer)
    return pl.pallas_call(
        functools.partial(_ple_kernel, final=final),
        grid=(n // tm,),
        in_specs=[
            pl.BlockSpec((tm, d), lambda i: (i, 0)),
            pl.BlockSpec((tm, PLE_DIM), lambda i: (i, 0)),
            pl.BlockSpec((1, d), lambda i: (0, 0)),
            full(big_w["w_ple_gate"]), full(big_w["w_ple_proj"]),
            pl.BlockSpec((1, d), lambda i: (0, 0)),
        ],
        out_specs=pl.BlockSpec((tm, d), lambda i: (i, 0)),
        out_shape=jax.ShapeDtypeStruct((n, d), F32),
        compiler_params=_cparams(("parallel",)),
        name="ple",
    )(x, p, w["ple_norm"], big_w["w_ple_gate"], big_w["w_ple_proj"], final_norm)


def _rope_tables(pos, rows):
    half = QK_ROPE // 2
    inv = ROPE_BASE ** (-jnp.arange(half, dtype=F32) / half)
    ang = pos.astype(F32)[:, None] * inv[None, :]
    cos, sin = jnp.cos(ang), jnp.sin(ang)
    z = lambda w_: jnp.zeros((pos.shape[0], w_), F32)
    cos_t = jnp.concatenate([jnp.ones((pos.shape[0], QK_NOPE), F32), cos, cos, z(32)], axis=1)
    sin_hi = jnp.concatenate([z(QK_NOPE + half), sin, z(32)], axis=1)
    sin_lo = jnp.concatenate([z(QK_NOPE), -sin, z(half + 32)], axis=1)
    reps = max(1, rows // pos.shape[0])
    return tuple(jnp.tile(t, (reps, 1)) for t in (cos_t, sin_hi, sin_lo))


def _layer_weights(i, p):
    row = lambda v: v.reshape(1, -1).astype(F32)
    w_in = p["w_in"][i]
    off_z, off_xbc, off_dt, off_q, off_kv = 2048, 5120, 5152, 5664, 5952
    z = w_in[:, :off_z]
    xbc = w_in[:, off_z:off_xbc]
    dt = w_in[:, off_xbc:off_dt]
    q_lat = w_in[:, off_dt:off_q]
    kv_lat = w_in[:, off_q:off_q + KV_LORA]
    k_rope = w_in[:, off_q + KV_LORA:off_kv]
    gates = w_in[:, off_kv:]
    w_main = jnp.concatenate([z, xbc, gates, q_lat], axis=1).astype(BF16)
    w_main_x = jnp.concatenate([w_main, jnp.zeros((D_MODEL, C_DTX - MAIN_DIM), BF16),
                                jnp.repeat(dt, SSM_HEAD_DIM, axis=1).astype(BF16)], axis=1)
    zeros32 = jnp.zeros((D_MODEL, 32), F32)
    small = jnp.concatenate([kv_lat, dt, zeros32, k_rope, zeros32], axis=1)

    lane_pad = lambda v: jnp.pad(v.astype(F32), (0, LANES - v.shape[0])).reshape(1, LANES)
    a_neg = -jnp.exp(p["a_log"][i].astype(F32))

    w_qb = p["w_qb"][i].reshape(Q_LORA, MLA_HEADS, QK_NOPE + QK_ROPE)
    w_qb_pad = jnp.pad(w_qb, ((0, 0), (0, 0), (0, HEAD_PAD - QK_NOPE - QK_ROPE)))
    w_uk = p["w_uk"][i]
    wk_nope = jnp.pad(w_uk, ((0, 0), (0, 0), (0, HEAD_PAD - QK_NOPE))).reshape(KV_LORA, -1)
    place = jnp.zeros((LANES, MLA_HEADS, HEAD_PAD), F32)
    j = jnp.arange(QK_ROPE)
    place = place.at[S_ROPE - KV_LORA + j, :, QK_NOPE + j].set(1.0).reshape(LANES, -1)
    w_abs = jnp.pad(jnp.transpose(w_uk, (1, 2, 0)), ((0, 0), (0, HEAD_PAD - QK_NOPE), (0, 0)))
    w_uv = p["w_uv"][i]
    eye = jnp.eye(MLA_HEADS, dtype=F32)
    w_uv_bd = (jnp.transpose(w_uv, (1, 0, 2))[:, :, None, :] * eye[:, None, :, None]).reshape(
        MLA_HEADS * KV_LORA, MLA_HEADS * V_HEAD)

    return {
        "ffn1_norm": row(p["ffn1_norm"][i]), "ffn2_norm": row(p["ffn2_norm"][i]),
        "mix_norm": row(p["mix_norm"][i]),
        "w_main": w_main, "w_main_x": w_main_x,
        "w_small": small.astype(BF16),
        "conv_w": p["conv_w"][i].astype(F32), "conv_b": row(p["conv_b"][i]),
        "dt_bias_l": lane_pad(p["dt_bias"][i]), "a_neg_l": lane_pad(a_neg * math.log2(math.e)),
        "dt_bias_x": row(jnp.repeat(p["dt_bias"][i], SSM_HEAD_DIM)),
        "a_neg_x": row(jnp.repeat(a_neg, SSM_HEAD_DIM)),
        "d_skip_x": row(jnp.repeat(p["d_skip"][i], SSM_HEAD_DIM)),
        "ssm_norm": row(p["ssm_norm"][i]),
        "q_norm": row(p["q_norm"][i]), "kv_norm": row(p["kv_norm"][i]),
        "w_qb_pad": w_qb_pad.reshape(Q_LORA, -1).astype(BF16),
        "wk_full": jnp.concatenate([wk_nope, place], axis=0).astype(BF16),
        "w_uv_t": w_uv.reshape(KV_LORA, -1).T.astype(BF16),
        "w_abs": w_abs.astype(BF16),
        "w_uv_bd": w_uv_bd.astype(BF16),
        "ple_norm": row(p["ple_norm"][i]),
    }


def _trunk_layer(x, p_l, tabs, conv_prev, ssm_prev, w, ffn_w, layer, bs, seq, final_norm, final, paged):
    x = ffn_half_step(x, w["ffn1_norm"], ffn_w["ffn1_w_gu"], ffn_w["ffn1_w_down"], layer)
    if paged is None:
        proj, small = in_project(x, w["mix_norm"], w["w_main"], w["w_small"], BF16, (512, 256, 128))
    else:
        proj, small = in_project(x, w["mix_norm"], w["w_main_x"], w["w_small"], F32, (256, 128))
    if paged is None:
        y_ssm, new_conv, new_ssm = ssm_prefill(proj, small, conv_prev, ssm_prev, w, bs, seq)
        c_kv, k_rope, kfull, vt = kv_project(small, w, tabs, True)
        q = q_project(proj, w, tabs, False)
        y_mla = prefill_attention(q, kfull, vt, bs, seq)
    else:
        cache_kv, cache_kr_t, page_table, new_ssm_stack = paged
        y_ssm, new_conv, new_ssm = ssm_step(proj, conv_prev, ssm_prev, new_ssm_stack, layer, w, bs, seq)
        c_kv, k_rope = kv_project(small, w, tabs, False)
        q, qa = q_project(proj, w, tabs, True)
        o_lat = decode_attention(qa, q, c_kv, k_rope, cache_kv, cache_kr_t, page_table, layer, bs, seq)
        y_mla = norm_matmul(o_lat, None, w["w_uv_bd"], BF16, norm=False, tm_prefs=(256, 128))
    x = merge(x, y_ssm, y_mla, proj, ffn_w, layer)
    x = ffn_half_step(x, w["ffn2_norm"], ffn_w["ffn2_w_gu"], ffn_w["ffn2_w_down"], layer)
    x = ple(x, p_l, w, ffn_w, layer, final_norm, final)
    return x, c_kv, k_rope, new_conv, new_ssm


def kernel(x_prompt, x_sample, cache_kv, cache_kr, state_conv, state_ssm, page_table, p_prompt, p_sample,
           ffn1_norm, ffn1_w_gu, ffn1_w_down, mix_norm, w_in, conv_w, conv_b, dt_bias, a_log, d_skip,
           ssm_norm, q_norm, w_qb, kv_norm, w_uk, w_uv, w_br_ssm, w_br_mla, w_out,
           ffn2_norm, ffn2_w_gu, ffn2_w_down, ple_norm, w_ple_gate, w_ple_proj, final_norm):
    params = dict(ffn1_norm=ffn1_norm, ffn1_w_gu=ffn1_w_gu, ffn1_w_down=ffn1_w_down, mix_norm=mix_norm,
                  w_in=w_in, conv_w=conv_w, conv_b=conv_b, dt_bias=dt_bias, a_log=a_log, d_skip=d_skip,
                  ssm_norm=ssm_norm, q_norm=q_norm, w_qb=w_qb, kv_norm=kv_norm, w_uk=w_uk, w_uv=w_uv,
                  w_br_ssm=w_br_ssm, w_br_mla=w_br_mla, w_out=w_out, ffn2_norm=ffn2_norm,
                  ffn2_w_gu=ffn2_w_gu, ffn2_w_down=ffn2_w_down, ple_norm=ple_norm,
                  w_ple_gate=w_ple_gate, w_ple_proj=w_ple_proj)
    depth = w_in.shape[0]
    bp, sp, d = x_prompt.shape
    bsm, ss, _ = x_sample.shape
    past_len = page_table.shape[1] * cache_kv.shape[2]
    np_, ns = bp * sp, bsm * ss

    tabs_p = _rope_tables(jnp.arange(sp), sp)
    tabs_s = _rope_tables(past_len + jnp.arange(ss), _pick(ns, (256, 128)))
    conv0 = jnp.zeros((bp, D_CONV - 1, CONV_DIM), F32)
    ssm0 = jnp.zeros((bp, SSM_HEADS, SSM_HEAD_DIM, SSM_STATE), F32)
    fnorm = final_norm.reshape(1, -1).astype(F32)

    cache_kr_t = jnp.swapaxes(cache_kr, 2, 3)
    ffn_w = {name: cast_bf16(params[name])
             for name in ("ffn1_w_gu", "ffn1_w_down", "ffn2_w_gu", "ffn2_w_down", "w_br_ssm",
                          "w_br_mla", "w_out", "w_ple_gate", "w_ple_proj")}

    hp = x_prompt.reshape(np_, d)
    hs = x_sample.reshape(ns, d)
    outs_p, outs_s = [], []
    new_ssm_stack = None
    for i in range(depth):
        last = i == depth - 1
        w = _layer_weights(i, params)
        hs, *rest_s = _trunk_layer(hs, p_sample[i].reshape(ns, PLE_DIM), tabs_s, state_conv,
                                   state_ssm, w, ffn_w, i, bsm, ss, fnorm, last,
                                   (cache_kv, cache_kr_t, page_table, new_ssm_stack))
        hp, *rest_p = _trunk_layer(hp, p_prompt[i].reshape(np_, PLE_DIM), tabs_p, conv0, ssm0, w,
                                   ffn_w, i, bp, sp, fnorm, last, None)
        new_ssm_stack = rest_s[3]
        outs_p.append(rest_p)
        outs_s.append(rest_s)

    stack = lambda outs, k, shape: jnp.stack([o[k] for o in outs]).reshape((depth,) + shape)
    return (
        hp.reshape(bp, sp, d), hs.reshape(bsm, ss, d),
        stack(outs_p, 0, (bp, sp, KV_LORA)), stack(outs_p, 1, (bp, sp, QK_ROPE)),
        stack(outs_p, 2, (bp, D_CONV - 1, CONV_DIM)),
        stack(outs_p, 3, (bp, SSM_HEADS, SSM_HEAD_DIM, SSM_STATE)),
        stack(outs_s, 0, (bsm, ss, KV_LORA)), stack(outs_s, 1, (bsm, ss, QK_ROPE)),
        stack(outs_s, 2, (bsm, D_CONV - 1, CONV_DIM)),
        new_ssm_stack,
    )
```

```python
import functools
import math

import jax
import jax.numpy as jnp
from jax import lax
from jax.experimental import pallas as pl
from jax.experimental.pallas import tpu as pltpu

F32 = jnp.float32
BF16 = jnp.bfloat16

D_MODEL = 1024
D_FF = 2816
D_INNER = 2048
SSM_HEAD_DIM = 64
SSM_HEADS = 32
SSM_GROUPS = 4
HEADS_PER_GROUP = SSM_HEADS // SSM_GROUPS
SSM_STATE = 128
GROUP_CH = D_INNER // SSM_GROUPS
D_CONV = 4
CONV_DIM = D_INNER + 2 * SSM_GROUPS * SSM_STATE
SSD_CHUNK = 128
MLA_HEADS = 16
Q_LORA = 512
KV_LORA = 256
QK_NOPE = 64
QK_ROPE = 32
V_HEAD = 64
ROPE_BASE = 10000.0
ATTN_SCALE = (QK_NOPE + QK_ROPE) ** -0.5
SCORE_SCALE = ATTN_SCALE * math.log2(math.e)
PLE_DIM = 256
EPS = 1e-6
LANES = 128
HEAD_PAD = 128
DECODE_CHUNK_PAGES = 4
NEG_BIG = -1e30

C_Z, C_XS, C_BC, C_GS, C_GM, C_Q = 0, 2048, 4096, 5120, 6144, 7168
MAIN_DIM = 7680
C_DTX = 8192
MAIN_DIM_X = C_DTX + D_INNER
SMALL_DIM = 384
S_DT, S_ROPE = 256, 320

VMEM_LIMIT = 56 * 1024 * 1024


def _cparams(sem):
    return pltpu.CompilerParams(dimension_semantics=sem, vmem_limit_bytes=VMEM_LIMIT)


def _pick(n, prefs):
    for p in prefs:
        if n % p == 0:
            return p
    return n


def _dot(a, b):
    return jnp.dot(a, b, preferred_element_type=F32)


def _dot_nt(a, b):
    return lax.dot_general(a, b, (((1,), (1,)), ((), ())), preferred_element_type=F32)


def _silu(x):
    return x * jax.nn.sigmoid(x)


def _softplus(x):
    return jnp.maximum(x, 0.0) + jnp.log1p(jnp.exp(-jnp.abs(x)))


def _rms(x, g):
    return x * lax.rsqrt(jnp.mean(x * x, axis=-1, keepdims=True) + EPS) * g


def _nm_kernel(x_ref, g_ref, w_ref, o_ref, u_sc, *, norm):
    @pl.when(pl.program_id(1) == 0)
    def _():
        x = x_ref[...].astype(F32)
        if norm:
            x = _rms(x, g_ref[...])
        u_sc[...] = x.astype(BF16)

    o_ref[...] = _dot(u_sc[...], w_ref[...]).astype(o_ref.dtype)


def norm_matmul(x, g, w, out_dtype, *, norm=True, col_block=0, tm_prefs=(1024, 512, 256, 128)):
    n = x.shape[0]
    k, nout = w.shape
    tm = _pick(n, tm_prefs)
    tn = _pick(nout, (1280, 1024, 512, 384, 256, 128))
    if not norm:
        g = jnp.ones((1, k), F32)
    return pl.pallas_call(
        functools.partial(_nm_kernel, norm=norm),
        grid=(n // tm, nout // tn),
        in_specs=[
            pl.BlockSpec((tm, k), lambda i, j: (i, col_block)),
            pl.BlockSpec((1, k), lambda i, j: (0, 0)),
            pl.BlockSpec((k, tn), lambda i, j: (0, j)),
        ],
        out_specs=pl.BlockSpec((tm, tn), lambda i, j: (i, j)),
        out_shape=jax.ShapeDtypeStruct((n, nout), out_dtype),
        scratch_shapes=[pltpu.VMEM((tm, k), BF16)],
        compiler_params=_cparams(("parallel", "arbitrary")),
        name="norm_matmul",
    )(x, g, w)


def _resident(shape):
    return pl.BlockSpec(shape, lambda *_: (0,) * len(shape), pipeline_mode=pl.Buffered(1))


def _resident_layer(stacked_shape, layer):
    return pl.BlockSpec((None,) + tuple(stacked_shape[1:]),
                        lambda *_: (layer,) + (0,) * (len(stacked_shape) - 1),
                        pipeline_mode=pl.Buffered(1))


def _cast_kernel(x_ref, o_ref):
    o_ref[...] = x_ref[...].astype(o_ref.dtype)


def cast_bf16(x):
    depth, rows, cols = x.shape
    tr = _pick(rows, (256, 128))
    return pl.pallas_call(
        _cast_kernel,
        grid=(depth, rows // tr),
        in_specs=[pl.BlockSpec((1, tr, cols), lambda l, i: (l, i, 0))],
        out_specs=pl.BlockSpec((1, tr, cols), lambda l, i: (l, i, 0)),
        out_shape=jax.ShapeDtypeStruct(x.shape, BF16),
        compiler_params=_cparams(("parallel", "parallel")),
        name="cast_bf16",
    )(x)


def _in_proj_kernel(x_ref, g_ref, wm_ref, ws_ref, om_ref, os_ref, *, chunk):
    u = _rms(x_ref[...], g_ref[...]).astype(BF16)
    for c0 in range(0, wm_ref.shape[1], chunk):
        om_ref[:, c0:c0 + chunk] = _dot(u, wm_ref[:, c0:c0 + chunk]).astype(om_ref.dtype)
    os_ref[...] = _dot(u, ws_ref[...])


def in_project(x, g, w_main, w_small, main_dtype, tm_prefs):
    n, d = x.shape
    nmain = w_main.shape[1]
    tm = _pick(n, tm_prefs)
    return pl.pallas_call(
        functools.partial(_in_proj_kernel, chunk=_pick(nmain, (1280, 1024, 512))),
        grid=(n // tm,),
        in_specs=[
            pl.BlockSpec((tm, d), lambda i: (i, 0)),
            pl.BlockSpec((1, d), lambda i: (0, 0)),
            _resident(w_main.shape),
            _resident(w_small.shape),
        ],
        out_specs=[pl.BlockSpec((tm, nmain), lambda i: (i, 0)),
                   pl.BlockSpec((tm, SMALL_DIM), lambda i: (i, 0))],
        out_shape=[jax.ShapeDtypeStruct((n, nmain), main_dtype),
                   jax.ShapeDtypeStruct((n, SMALL_DIM), F32)],
        compiler_params=_cparams(("parallel",)),
        name="in_project",
    )(x, g, w_main, w_small)


def _ffn_kernel(x_ref, g_ref, wgu_ref, wd_ref, o_ref):
    x = x_ref[...]
    dff = wd_ref.shape[0]
    u = _rms(x, g_ref[...]).astype(BF16)
    a = _dot(u, wgu_ref[:, :dff])
    b = _dot(u, wgu_ref[:, dff:])
    h = (_silu(a) * b).astype(BF16)
    o_ref[...] = x + 0.5 * _dot(h, wd_ref[...])


def ffn_half_step(x, g, w_gu, w_down, layer):
    n, d = x.shape
    tm = _pick(n, (512, 256, 128))
    return pl.pallas_call(
        _ffn_kernel,
        grid=(n // tm,),
        in_specs=[
            pl.BlockSpec((tm, d), lambda i: (i, 0)),
            pl.BlockSpec((1, d), lambda i: (0, 0)),
            _resident_layer(w_gu.shape, layer),
            _resident_layer(w_down.shape, layer),
        ],
        out_specs=pl.BlockSpec((tm, d), lambda i: (i, 0)),
        out_shape=jax.ShapeDtypeStruct((n, d), F32),
        compiler_params=_cparams(("parallel",)),
        name="ffn_half_step",
    )(x, g, w_gu, w_down)


def _ffn_ple_kernel(x_ref, g_ref, wgu_ref, wd_ref, p_ref, pg_ref, wg_ref, wp_ref, fg_ref, o_ref, *, final):
    x = x_ref[...]
    dff = wd_ref.shape[0]
    u = _rms(x, g_ref[...]).astype(BF16)
    a = _dot(u, wgu_ref[:, :dff])
    b = _dot(u, wgu_ref[:, dff:])
    h = (_silu(a) * b).astype(BF16)
    x = x + 0.5 * _dot(h, wd_ref[...])
    gate = jax.nn.sigmoid(_dot(_rms(x, pg_ref[...]).astype(BF16), wg_ref[...]))
    x = x + gate * _dot(p_ref[...].astype(BF16), wp_ref[...])
    if final:
        x = _rms(x, fg_ref[...])
    o_ref[...] = x


def ffn_ple_step(x, p, w, big_w, layer, final_norm, final):
    n, d = x.shape
    tm = _pick(n, (512, 256, 128))
    vec = pl.BlockSpec((1, d), lambda i: (0, 0))
    names = ("ffn2_w_gu", "ffn2_w_down", "w_ple_gate", "w_ple_proj")
    res = {k: _resident_layer(big_w[k].shape, layer) for k in names}
    return pl.pallas_call(
        functools.partial(_ffn_ple_kernel, final=final),
        grid=(n // tm,),
        in_specs=[
            pl.BlockSpec((tm, d), lambda i: (i, 0)), vec,
            res["ffn2_w_gu"], res["ffn2_w_down"],
            pl.BlockSpec((tm, PLE_DIM), lambda i: (i, 0)), vec,
            res["w_ple_gate"], res["w_ple_proj"], vec,
        ],
        out_specs=pl.BlockSpec((tm, d), lambda i: (i, 0)),
        out_shape=jax.ShapeDtypeStruct((n, d), F32),
        compiler_params=_cparams(("parallel",)),
        name="ffn_ple_step",
    )(x, w["ffn2_norm"], big_w["ffn2_w_gu"], big_w["ffn2_w_down"], p, w["ple_norm"],
      big_w["w_ple_gate"], big_w["w_ple_proj"], final_norm)


def _causal_conv(xpad_sc, rows, w_ref, b_ref):
    w = w_ref[...]
    win = xpad_sc[0:8 + rows, :]
    acc = win[8:, :] * w[D_CONV - 1:D_CONV, :]
    for k in range(D_CONV - 1):
        shifted = pltpu.roll(win, D_CONV - 1 - k, axis=0)[8:, :]
        acc = acc + shifted * w[k:k + 1, :]
    return _silu(acc + b_ref[...])


def _gate_and_group_norm(y, z, norm_w):
    y = y * _silu(z)
    parts = []
    for g in range(SSM_GROUPS):
        yg = y[:, g * GROUP_CH:(g + 1) * GROUP_CH]
        parts.append(yg * lax.rsqrt(jnp.mean(yg * yg, axis=-1, keepdims=True) + EPS))
    return jnp.concatenate(parts, axis=1) * norm_w


def _cumsum_rows(a):
    rows = a.shape[0]
    row = lax.broadcasted_iota(jnp.int32, a.shape, 0)
    s = 1
    while s < rows:
        a = a + jnp.where(row >= s, pltpu.roll(a, s, axis=0), 0.0)
        s *= 2
    return a


def _ssm_prefill_kernel(z_ref, xs_ref, bc_ref, dt_ref, cprev_ref, sprev_ref, convw_ref, convb_ref,
                        dtb_ref, aneg_ref, dskip_ref, norm_ref,
                        y_ref, nconv_ref, nssm_ref, xpad_sc, h_sc):
    c = pl.program_id(1)
    t = SSD_CHUNK
    hp = SSM_HEAD_DIM

    @pl.when(c == 0)
    def _():
        xpad_sc[0:8, :] = jnp.zeros((8, CONV_DIM), F32)
        xpad_sc[5:8, :] = cprev_ref[0]
        h_sc[...] = sprev_ref[0].reshape(D_INNER, SSM_STATE)

    xpad_sc[8:8 + t, 0:D_INNER] = xs_ref[...].astype(F32)
    xpad_sc[8:8 + t, D_INNER:CONV_DIM] = bc_ref[...].astype(F32)
    conv = _causal_conv(xpad_sc, t, convw_ref, convb_ref)
    tail = xpad_sc[8 + t - 3:8 + t, :]
    xpad_sc[5:8, :] = tail

    xs = conv[:, :D_INNER]
    gs = SSM_GROUPS * SSM_STATE
    bm = conv[:, D_INNER:D_INNER + gs]
    cm = conv[:, D_INNER + gs:]

    dt = _softplus(dt_ref[...] + dtb_ref[...])
    acum = _cumsum_rows(dt * aneg_ref[...])
    alast = acum[t - 1:t, :]
    acum_t = acum.T
    dt_t = dt.T
    wst_t = (dt * jnp.exp2(alast - acum)).T
    xs_t = xs.T

    def row_bcast(a, r, rows):
        one = jnp.broadcast_to(a[r:r + 1, :], (8, a.shape[1]))
        return jnp.concatenate([one] * (rows // 8), axis=0)

    tri = (lax.broadcasted_iota(jnp.int32, (t, t), 0) >= lax.broadcasted_iota(jnp.int32, (t, t), 1))
    low_half = lax.broadcasted_iota(jnp.int32, (t, 2 * hp), 1) < hp

    y_parts = []
    for g in range(SSM_GROUPS):
        bg = bm[:, g * SSM_STATE:(g + 1) * SSM_STATE]
        cg = cm[:, g * SSM_STATE:(g + 1) * SSM_STATE]
        cb = _dot_nt(cg.astype(BF16), bg.astype(BF16))
        for pair in range(HEADS_PER_GROUP // 2):
            h0 = g * HEADS_PER_GROUP + 2 * pair
            lhs = []
            for h in (h0, h0 + 1):
                colb = jnp.broadcast_to(acum[:, h:h + 1], (t, t))
                decay = jnp.where(tri, jnp.exp2(colb - row_bcast(acum_t, h, t)), 0.0)
                m = cb * decay * row_bcast(dt_t, h, t)
                ce = cg * jnp.exp2(colb)
                lhs.append(jnp.concatenate([m, ce], axis=1))
            lhs = jnp.concatenate(lhs, axis=0).astype(BF16)
            r0 = h0 * hp
            rhs_t = jnp.concatenate([xs_t[r0:r0 + 2 * hp, :], h_sc[r0:r0 + 2 * hp, :]],
                                    axis=1).astype(BF16)
            out = _dot_nt(lhs, rhs_t)
            y_parts.append(jnp.where(low_half, out[:t, :], out[t:, :]))
    y = jnp.concatenate(y_parts, axis=1) + dskip_ref[...] * xs
    y_ref[...] = _gate_and_group_norm(y, z_ref[...].astype(F32), norm_ref[...]).astype(y_ref.dtype)

    elast = jnp.exp2(alast)
    for g in range(SSM_GROUPS):
        bg = bm[:, g * SSM_STATE:(g + 1) * SSM_STATE].astype(BF16)
        xw = []
        for hh in range(HEADS_PER_GROUP):
            h = g * HEADS_PER_GROUP + hh
            xw.append(xs_t[h * hp:(h + 1) * hp, :] * row_bcast(wst_t, h, hp))
        dh = _dot(jnp.concatenate(xw, axis=0).astype(BF16), bg)
        for hh in range(HEADS_PER_GROUP):
            h = g * HEADS_PER_GROUP + hh
            rows = slice(h * hp, (h + 1) * hp)
            h_sc[rows, :] = h_sc[rows, :] * elast[:, h:h + 1] + dh[hh * hp:(hh + 1) * hp, :]

    @pl.when(c == pl.num_programs(1) - 1)
    def _():
        nconv_ref[0] = tail
        nssm_ref[0] = h_sc[...].reshape(SSM_HEADS, SSM_HEAD_DIM, SSM_STATE)


def ssm_prefill(proj, small, conv_prev, ssm_prev, w, bs, seq):
    t = SSD_CHUNK
    nc = seq // t
    n = bs * seq
    row = lambda b, c: b * nc + c
    vec = lambda width: pl.BlockSpec((1, width), lambda b, c: (0, 0))
    return pl.pallas_call(
        _ssm_prefill_kernel,
        grid=(bs, nc),
        in_specs=[
            pl.BlockSpec((t, D_INNER), lambda b, c: (row(b, c), C_Z // D_INNER)),
            pl.BlockSpec((t, D_INNER), lambda b, c: (row(b, c), C_XS // D_INNER)),
            pl.BlockSpec((t, 1024), lambda b, c: (row(b, c), C_BC // 1024)),
            pl.BlockSpec((t, LANES), lambda b, c: (row(b, c), S_DT // LANES)),
            pl.BlockSpec((1, D_CONV - 1, CONV_DIM), lambda b, c: (b, 0, 0)),
            pl.BlockSpec((1, SSM_HEADS, SSM_HEAD_DIM, SSM_STATE), lambda b, c: (b, 0, 0, 0)),
            pl.BlockSpec((D_CONV, CONV_DIM), lambda b, c: (0, 0)),
            vec(CONV_DIM), vec(LANES), vec(LANES), vec(D_INNER), vec(D_INNER),
        ],
        out_specs=[
            pl.BlockSpec((t, D_INNER), lambda b, c: (row(b, c), 0)),
            pl.BlockSpec((1, D_CONV - 1, CONV_DIM), lambda b, c: (b, 0, 0)),
            pl.BlockSpec((1, SSM_HEADS, SSM_HEAD_DIM, SSM_STATE), lambda b, c: (b, 0, 0, 0)),
        ],
        out_shape=[
            jax.ShapeDtypeStruct((n, D_INNER), F32),
            jax.ShapeDtypeStruct((bs, D_CONV - 1, CONV_DIM), F32),
            jax.ShapeDtypeStruct((bs, SSM_HEADS, SSM_HEAD_DIM, SSM_STATE), F32),
        ],
        scratch_shapes=[pltpu.VMEM((t + 8, CONV_DIM), F32), pltpu.VMEM((D_INNER, SSM_STATE), F32)],
        compiler_params=_cparams(("parallel", "arbitrary")),
        name="ssm_prefill",
    )(proj, proj, proj, small, conv_prev, ssm_prev, w["conv_w"], w["conv_b"], w["dt_bias_l"],
      w["a_neg_l"], w["d_skip_x"], w["ssm_norm"])


def _ssm_step_kernel(z_ref, xs_ref, bc_ref, dtx_ref, cprev_ref, sprev_ref, convw_ref, convb_ref,
                     dtbx_ref, anegx_ref, dskip_ref, norm_ref, *rest):
    y_ref, nconv_ref, nssm_ref, xpad_sc, xw_sc, bpad_sc = rest[-6:]
    seq = xs_ref.shape[0]
    hp = SSM_HEAD_DIM
    gs = SSM_GROUPS * SSM_STATE

    @pl.when(pl.program_id(0) == 0)
    def _():
        xw_sc[...] = jnp.zeros_like(xw_sc)
        bpad_sc[...] = jnp.zeros_like(bpad_sc)
        xpad_sc[0:8, :] = jnp.zeros((8, CONV_DIM), F32)

    xpad_sc[5:8, :] = cprev_ref[0, 0]
    xpad_sc[8:8 + seq, 0:D_INNER] = xs_ref[...]
    xpad_sc[8:8 + seq, D_INNER:CONV_DIM] = bc_ref[...]
    conv = _causal_conv(xpad_sc, seq, convw_ref, convb_ref)
    nconv_ref[0] = xpad_sc[8 + seq - 3:8 + seq, :]

    xs = conv[:, :D_INNER]
    bm = conv[:, D_INNER:D_INNER + gs]
    cm = conv[:, D_INNER + gs:]

    dtx = _softplus(dtx_ref[...] + dtbx_ref[...])
    acum = _cumsum_rows(dtx * anegx_ref[...])
    alast = acum[seq - 1:seq, :]
    xd = xs * dtx

    c_all = jnp.concatenate([cm[:, g * SSM_STATE:(g + 1) * SSM_STATE] for g in range(SSM_GROUPS)],
                            axis=0).astype(BF16)
    y_off = []
    for g in range(SSM_GROUPS):
        hg = sprev_ref[0, 0, g * HEADS_PER_GROUP:(g + 1) * HEADS_PER_GROUP].reshape(GROUP_CH, SSM_STATE)
        y_off.append(_dot_nt(c_all, hg.astype(BF16))[g * seq:(g + 1) * seq, :])
    y = jnp.concatenate(y_off, axis=1) * jnp.exp(acum)

    ones = jnp.ones((SSM_STATE, GROUP_CH), BF16)
    cbx = []
    for g in range(SSM_GROUPS):
        bg = bm[:, g * SSM_STATE:(g + 1) * SSM_STATE]
        cg = cm[:, g * SSM_STATE:(g + 1) * SSM_STATE]
        prod = jnp.concatenate([cg * bg[s:s + 1, :] for s in range(seq)], axis=0)
        hi = prod.astype(BF16)
        lo = (prod - hi.astype(F32)).astype(BF16)
        cbx.append(_dot(hi, ones) + _dot(lo, ones))
    row = lax.broadcasted_iota(jnp.int32, (seq, D_INNER), 0)
    for s in range(seq):
        cb_s = jnp.concatenate([cbx[g][s * seq:(s + 1) * seq, :] for g in range(SSM_GROUPS)], axis=1)
        decay = jnp.where(row >= s, jnp.exp(acum - acum[s:s + 1, :]), 0.0)
        y = y + cb_s * decay * xd[s:s + 1, :]

    y = y + dskip_ref[...] * xs
    y_ref[...] = _gate_and_group_norm(y, z_ref[...], norm_ref[...]).astype(y_ref.dtype)

    xw_sc[0:seq, :] = xd * jnp.exp(alast - acum)
    bpad_sc[0:seq, :] = bm
    xw_t = xw_sc[...].T
    elast = jnp.exp(alast)
    for g in range(SSM_GROUPS):
        dh = _dot(xw_t[g * GROUP_CH:(g + 1) * GROUP_CH, :].astype(BF16),
                  bpad_sc[:, g * SSM_STATE:(g + 1) * SSM_STATE].astype(BF16))
        for hh in range(HEADS_PER_GROUP):
            h = g * HEADS_PER_GROUP + hh
            nssm_ref[0, 0, h] = (sprev_ref[0, 0, h] * elast[:, h * hp:h * hp + 1]
                                 + dh[hh * hp:(hh + 1) * hp, :])


def ssm_step(proj, state_conv, state_ssm, new_ssm_stack, layer, w, bs, seq):
    n = bs * seq
    vec = lambda width: pl.BlockSpec((1, width), lambda b: (0, 0))
    state_block = (1, 1, SSM_HEADS, SSM_HEAD_DIM, SSM_STATE)
    in_specs = [
        pl.BlockSpec((seq, D_INNER), lambda b: (b, C_Z // D_INNER)),
        pl.BlockSpec((seq, D_INNER), lambda b: (b, C_XS // D_INNER)),
        pl.BlockSpec((seq, 1024), lambda b: (b, C_BC // 1024)),
        pl.BlockSpec((seq, D_INNER), lambda b: (b, C_DTX // D_INNER)),
        pl.BlockSpec((1, 1, D_CONV - 1, CONV_DIM), lambda b: (layer, b, 0, 0)),
        pl.BlockSpec(state_block, lambda b: (layer, b, 0, 0, 0)),
        pl.BlockSpec((D_CONV, CONV_DIM), lambda b: (0, 0)),
        vec(CONV_DIM), vec(D_INNER), vec(D_INNER), vec(D_INNER), vec(D_INNER),
    ]
    args = [proj, proj, proj, proj, state_conv, state_ssm, w["conv_w"], w["conv_b"], w["dt_bias_x"],
            w["a_neg_x"], w["d_skip_x"], w["ssm_norm"]]
    aliases = {}
    if new_ssm_stack is not None:
        in_specs.append(pl.BlockSpec(memory_space=pl.ANY))
        args.append(new_ssm_stack)
        aliases = {len(args) - 1: 2}
    return pl.pallas_call(
        _ssm_step_kernel,
        grid=(bs,),
        in_specs=in_specs,
        out_specs=[
            pl.BlockSpec((seq, D_INNER), lambda b: (b, 0)),
            pl.BlockSpec((1, D_CONV - 1, CONV_DIM), lambda b: (b, 0, 0)),
            pl.BlockSpec(state_block, lambda b: (layer, b, 0, 0, 0)),
        ],
        out_shape=[
            jax.ShapeDtypeStruct((n, D_INNER), F32),
            jax.ShapeDtypeStruct((bs, D_CONV - 1, CONV_DIM), F32),
            jax.ShapeDtypeStruct(state_ssm.shape, F32),
        ],
        scratch_shapes=[pltpu.VMEM((seq + 8, CONV_DIM), F32), pltpu.VMEM((LANES, D_INNER), F32),
                        pltpu.VMEM((LANES, SSM_GROUPS * SSM_STATE), F32)],
        input_output_aliases=aliases,
        compiler_params=_cparams(("arbitrary",)),
        name="ssm_step",
    )(*args)


def _rope(x, cos, sin_hi, sin_lo):
    return x * cos + pltpu.roll(x, 16, axis=1) * sin_hi + pltpu.roll(x, x.shape[1] - 16, axis=1) * sin_lo


def _kv_kernel(s_ref, g_ref, cos_ref, shi_ref, slo_ref, wk_ref, wv_ref,
               ckv_ref, kr_ref, *kv_out, with_kv):
    blk = s_ref[...]
    ckv = _rms(blk[:, :KV_LORA], g_ref[...])
    ckv_ref[...] = ckv
    tail = _rope(blk[:, KV_LORA:], cos_ref[...], shi_ref[...], slo_ref[...])
    kr_ref[...] = tail[:, S_ROPE - KV_LORA:S_ROPE - KV_LORA + QK_ROPE]
    if with_kv:
        kfull_ref, vt_ref = kv_out
        ckv16 = ckv.astype(BF16)
        lhs = jnp.concatenate([ckv16, tail.astype(BF16)], axis=1)
        kfull_ref[...] = _dot(lhs, wk_ref[...]).astype(kfull_ref.dtype)
        vt_ref[0] = _dot_nt(wv_ref[...], ckv16).astype(vt_ref.dtype)


def kv_project(small, w, tabs, with_kv):
    n = small.shape[0]
    tm = min(_pick(n, (512, 256, 128)), tabs[0].shape[0])
    ntab = tabs[0].shape[0] // tm
    tab = pl.BlockSpec((tm, LANES), lambda i: (i % ntab, 0))
    out_specs = [pl.BlockSpec((tm, KV_LORA), lambda i: (i, 0)), pl.BlockSpec((tm, QK_ROPE), lambda i: (i, 0))]
    out_shape = [jax.ShapeDtypeStruct((n, KV_LORA), F32), jax.ShapeDtypeStruct((n, QK_ROPE), F32)]
    if with_kv:
        out_specs += [pl.BlockSpec((tm, MLA_HEADS * HEAD_PAD), lambda i: (i, 0)),
                      pl.BlockSpec((1, MLA_HEADS * V_HEAD, tm), lambda i: (i, 0, 0))]
        out_shape += [jax.ShapeDtypeStruct((n, MLA_HEADS * HEAD_PAD), BF16),
                      jax.ShapeDtypeStruct((n // tm, MLA_HEADS * V_HEAD, tm), BF16)]
    return pl.pallas_call(
        functools.partial(_kv_kernel, with_kv=with_kv),
        grid=(n // tm,),
        in_specs=[
            pl.BlockSpec((tm, SMALL_DIM), lambda i: (i, 0)),
            pl.BlockSpec((1, KV_LORA), lambda i: (0, 0)),
            tab, tab, tab,
            pl.BlockSpec(w["wk_full"].shape, lambda i: (0, 0)),
            pl.BlockSpec(w["w_uv_t"].shape, lambda i: (0, 0)),
        ],
        out_specs=out_specs,
        out_shape=out_shape,
        compiler_params=_cparams(("parallel",)),
        name="kv_project",
    )(small, w["kv_norm"], *tabs, w["wk_full"], w["w_uv_t"])


def _q_kernel(x_ref, g_ref, cos_ref, shi_ref, slo_ref, wq_ref, *rest, absorb):
    u = _rms(x_ref[...].astype(F32), g_ref[...]).astype(BF16)
    q = _dot(u, wq_ref[...])
    tile = lambda r: jnp.concatenate([r[...]] * MLA_HEADS, axis=1)
    q = _rope(q, tile(cos_ref), tile(shi_ref), tile(slo_ref))
    if absorb:
        wabs_ref, q_ref, qa_ref = rest
        q_ref[...] = q.astype(q_ref.dtype)
        q16 = q.astype(BF16)
        for h in range(MLA_HEADS):
            qa_ref[:, h * KV_LORA:(h + 1) * KV_LORA] = _dot(
                q16[:, h * HEAD_PAD:(h + 1) * HEAD_PAD], wabs_ref[h]).astype(qa_ref.dtype)
    else:
        (q_ref,) = rest
        q_ref[...] = (q * SCORE_SCALE).astype(q_ref.dtype)


def q_project(proj, w, tabs, absorb):
    n = proj.shape[0]
    tm = min(_pick(n, (512, 256, 128)), tabs[0].shape[0])
    ntab = tabs[0].shape[0] // tm
    tab = pl.BlockSpec((tm, LANES), lambda i: (i % ntab, 0))
    qw = MLA_HEADS * HEAD_PAD
    in_specs = [
        pl.BlockSpec((tm, Q_LORA), lambda i: (i, C_Q // Q_LORA)),
        pl.BlockSpec((1, Q_LORA), lambda i: (0, 0)),
        tab, tab, tab,
        pl.BlockSpec((Q_LORA, qw), lambda i: (0, 0)),
    ]
    args = [proj, w["q_norm"], *tabs, w["w_qb_pad"]]
    if absorb:
        in_specs.append(pl.BlockSpec((MLA_HEADS, HEAD_PAD, KV_LORA), lambda i: (0, 0, 0)))
        args.append(w["w_abs"])
        out_specs = [pl.BlockSpec((tm, qw), lambda i: (i, 0)),
                     pl.BlockSpec((tm, MLA_HEADS * KV_LORA), lambda i: (i, 0))]
        out_shape = [jax.ShapeDtypeStruct((n, qw), F32),
                     jax.ShapeDtypeStruct((n, MLA_HEADS * KV_LORA), F32)]
    else:
        out_specs = pl.BlockSpec((tm, qw), lambda i: (i, 0))
        out_shape = jax.ShapeDtypeStruct((n, qw), BF16)
    return pl.pallas_call(
        functools.partial(_q_kernel, absorb=absorb),
        grid=(n // tm,),
        in_specs=in_specs,
        out_specs=out_specs,
        out_shape=out_shape,
        compiler_params=_cparams(("parallel",)),
        name="q_project",
    )(*args)


def _bcast_lanes(x, width):
    return jnp.concatenate([x] * (width // LANES), axis=1)


def _prefill_attn_kernel(q_ref, k_ref, vt_ref, o_ref, m_sc, l_sc, acc_sc):
    qi = pl.program_id(2)
    tq = q_ref.shape[0]
    tk = vt_ref.shape[2]
    nsplit, _, tc = acc_sc.shape
    head0 = lax.broadcasted_iota(jnp.int32, (2 * V_HEAD, tc), 0) < V_HEAD

    m_sc[...] = jnp.full_like(m_sc, NEG_BIG)
    l_sc[...] = jnp.zeros_like(l_sc)
    acc_sc[...] = jnp.zeros_like(acc_sc)

    def update(kis, masked):
        nkeys = [(c + 1) * tc if masked else tk for c in range(nsplit)]
        scores = []
        for ki in kis:
            start = pl.multiple_of(ki * tk, tk)
            for c in range(nsplit):
                for h in range(2):
                    qh = q_ref[c * tc:(c + 1) * tc, h * HEAD_PAD:(h + 1) * HEAD_PAD]
                    kh = k_ref[pl.ds(start, nkeys[c]), h * HEAD_PAD:(h + 1) * HEAD_PAD]
                    scores.append(_dot_nt(kh, qh))
        for n, ki in enumerate(kis):
            for c in range(nsplit):
                vt = vt_ref[ki, :, 0:nkeys[c]]
                alphas, pvs = [], []
                for h in range(2):
                    st = c * 2 + h
                    s = scores[n * 2 * nsplit + st]
                    if masked:
                        keep = (lax.broadcasted_iota(jnp.int32, (nkeys[c], tc), 0)
                                <= lax.broadcasted_iota(jnp.int32, (nkeys[c], tc), 1) + c * tc)
                        s = jnp.where(keep, s, NEG_BIG)
                    m_prev = m_sc[st]
                    m_next = jnp.maximum(m_prev, jnp.max(s, axis=0, keepdims=True))
                    p = jnp.exp2(s - m_next)
                    alpha = jnp.exp2(m_prev - m_next)
                    l_sc[st] = alpha * l_sc[st] + jnp.sum(p, axis=0, keepdims=True)
                    m_sc[st] = m_next
                    alphas.append(alpha)
                    pvs.append(_dot(vt, p.astype(BF16)))
                acc_sc[c] = (jnp.where(head0, alphas[0], alphas[1]) * acc_sc[c]
                             + jnp.where(head0, pvs[0], pvs[1]))

    def body(j, carry):
        update((2 * j, 2 * j + 1), False)
        return carry

    lax.fori_loop(0, qi // 2, body, 0)

    @pl.when(qi % 2 == 1)
    def _():
        update((qi - 1,), False)

    update((qi,), True)
    for c in range(nsplit):
        o_t = acc_sc[c] / jnp.where(head0, l_sc[2 * c], l_sc[2 * c + 1])
        o_ref[c * tc:(c + 1) * tc, :] = o_t.T.astype(o_ref.dtype)


def prefill_attention(q, kfull, vt, bs, seq):
    tk = vt.shape[2]
    tq = tk
    nq = seq // tq
    n = bs * seq
    nsplit = 2 if tq % (2 * LANES) == 0 else 1
    tc = tq // nsplit
    return pl.pallas_call(
        _prefill_attn_kernel,
        grid=(bs, MLA_HEADS // 2, nq),
        in_specs=[
            pl.BlockSpec((tq, 2 * HEAD_PAD), lambda b, hp, qi: (b * nq + qi, hp)),
            pl.BlockSpec((seq, 2 * HEAD_PAD), lambda b, hp, qi: (b, hp)),
            pl.BlockSpec((nq, 2 * V_HEAD, tk), lambda b, hp, qi: (b, hp, 0)),
        ],
        out_specs=pl.BlockSpec((tq, 2 * V_HEAD), lambda b, hp, qi: (b * nq + qi, hp)),
        out_shape=jax.ShapeDtypeStruct((n, MLA_HEADS * V_HEAD), BF16),
        scratch_shapes=[pltpu.VMEM((2 * nsplit, 1, tc), F32), pltpu.VMEM((2 * nsplit, 1, tc), F32),
                        pltpu.VMEM((nsplit, 2 * V_HEAD, tc), F32)],
        compiler_params=_cparams(("parallel", "parallel", "arbitrary")),
        name="prefill_attention",
    )(q, kfull, vt)


def _decode_attn_kernel(pt_ref, qa_ref, q_ref, cnew_ref, rnew_ref, kv_hbm, kr_hbm, o_ref,
                        kv_buf, kr_buf, kv_sem, kr_sem, *, layer, n_pages):
    b = pl.program_id(0)
    nb = pl.num_programs(0)
    seq = qa_ref.shape[0]
    rows = MLA_HEADS * seq
    page = kv_hbm.shape[2]
    slot = b % 2

    def page_copies(seq_idx, dst_slot, i):
        pid = pt_ref[seq_idx, i]
        return (pltpu.make_async_copy(kv_hbm.at[layer, pid],
                                      kv_buf.at[dst_slot, pl.ds(i * page, page), :],
                                      kv_sem.at[dst_slot]),
                pltpu.make_async_copy(kr_hbm.at[layer, pid],
                                      kr_buf.at[dst_slot, i],
                                      kr_sem.at[dst_slot]))

    def start_fetch(seq_idx, dst_slot):
        for i in range(n_pages):
            for cp in page_copies(seq_idx, dst_slot, i):
                cp.start()

    @pl.when(b == 0)
    def _():
        start_fetch(0, 0)

    @pl.when(b + 1 < nb)
    def _():
        start_fetch(b + 1, 1 - slot)

    qa = jnp.concatenate([qa_ref[:, h * KV_LORA:(h + 1) * KV_LORA] for h in range(MLA_HEADS)],
                         axis=0) * SCORE_SCALE
    qr = jnp.concatenate([q_ref[:, h * HEAD_PAD:(h + 1) * HEAD_PAD] for h in range(MLA_HEADS)],
                         axis=0)[:, QK_NOPE:QK_NOPE + QK_ROPE] * SCORE_SCALE
    qa = qa.astype(BF16)
    qr = qr.astype(BF16)

    for i in range(n_pages):
        for cp in page_copies(b, slot, i):
            cp.wait()

    cp_pages = min(n_pages, DECODE_CHUNK_PAGES)
    ck = cp_pages * page
    chunks, scores = [], []
    for p0 in range(0, n_pages, cp_pages):
        c = kv_buf[slot, pl.ds(p0 * page, ck), :].astype(BF16)
        r = jnp.concatenate([kr_buf[slot, i] for i in range(p0, p0 + cp_pages)], axis=1).astype(BF16)
        chunks.append(c)
        scores.append(_dot_nt(qa, c) + _dot(qr, r))
    pad = page - seq
    c_new = jnp.concatenate([cnew_ref[...], jnp.zeros((pad, KV_LORA), F32)], axis=0).astype(BF16)
    r_new = jnp.concatenate([rnew_ref[...], jnp.zeros((pad, QK_ROPE), F32)], axis=0).astype(BF16)
    s_new = _dot_nt(qa, c_new) + _dot_nt(qr, r_new)
    tok = lax.broadcasted_iota(jnp.int32, (rows, page), 0) % seq
    key = lax.broadcasted_iota(jnp.int32, (rows, page), 1)
    chunks.append(c_new)
    scores.append(jnp.where(key <= tok, s_new, NEG_BIG))

    m = jnp.max(scores[0], axis=1, keepdims=True)
    for s in scores[1:]:
        m = jnp.maximum(m, jnp.max(s, axis=1, keepdims=True))
    l = jnp.zeros((rows, 1), F32)
    half = rows // 2
    acc_a = jnp.zeros((half, KV_LORA), F32)
    acc_b = jnp.zeros((rows - half, KV_LORA), F32)
    for c, s in zip(chunks, scores):
        p = jnp.exp2(s - m)
        l = l + jnp.sum(p, axis=1, keepdims=True)
        p16 = p.astype(BF16)
        acc_a = acc_a + _dot(p16[:half], c)
        acc_b = acc_b + _dot(p16[half:], c)
    o = jnp.concatenate([acc_a, acc_b], axis=0) / l
    for h in range(MLA_HEADS):
        o_ref[:, h * KV_LORA:(h + 1) * KV_LORA] = o[h * seq:(h + 1) * seq, :].astype(o_ref.dtype)


def decode_attention(qa, q, ckv, kr, cache_kv, cache_kr_t, page_table, layer, bs, seq):
    n_pages = page_table.shape[1]
    page = cache_kv.shape[2]
    n = bs * seq
    grid_spec = pltpu.PrefetchScalarGridSpec(
        num_scalar_prefetch=1,
        grid=(bs,),
        in_specs=[
            pl.BlockSpec((seq, MLA_HEADS * KV_LORA), lambda b, pt: (b, 0)),
            pl.BlockSpec((seq, MLA_HEADS * HEAD_PAD), lambda b, pt: (b, 0)),
            pl.BlockSpec((seq, KV_LORA), lambda b, pt: (b, 0)),
            pl.BlockSpec((seq, QK_ROPE), lambda b, pt: (b, 0)),
            pl.BlockSpec(memory_space=pl.ANY),
            pl.BlockSpec(memory_space=pl.ANY),
        ],
        out_specs=pl.BlockSpec((seq, MLA_HEADS * KV_LORA), lambda b, pt: (b, 0)),
        scratch_shapes=[pltpu.VMEM((2, n_pages * page, KV_LORA), F32),
                        pltpu.VMEM((2, n_pages, QK_ROPE, page), F32),
                        pltpu.SemaphoreType.DMA((2,)), pltpu.SemaphoreType.DMA((2,))],
    )
    return pl.pallas_call(
        functools.partial(_decode_attn_kernel, layer=layer, n_pages=n_pages),
        grid_spec=grid_spec,
        out_shape=jax.ShapeDtypeStruct((n, MLA_HEADS * KV_LORA), F32),
        compiler_params=_cparams(("arbitrary",)),
        name="decode_attention",
    )(page_table, qa, q, ckv, kr, cache_kv, cache_kr_t)


def _merge_kernel(x_ref, ys_ref, ym_ref, gs_ref, gm_ref, ws_ref, wm_ref, wo_ref, o_ref):
    ms = _dot(ys_ref[...].astype(BF16), ws_ref[...])
    mm = _dot(ym_ref[...].astype(BF16), wm_ref[...])
    merged = jax.nn.sigmoid(gs_ref[...].astype(F32)) * ms + jax.nn.sigmoid(gm_ref[...].astype(F32)) * mm
    o_ref[...] = x_ref[...] + _dot(merged.astype(BF16), wo_ref[...])


def merge(x, y_ssm, y_mla, proj, w, layer):
    n, d = x.shape
    tm = _pick(n, (512, 256, 128))
    full = lambda a: _resident_layer(a.shape, layer)
    return pl.pallas_call(
        _merge_kernel,
        grid=(n // tm,),
        in_specs=[
            pl.BlockSpec((tm, d), lambda i: (i, 0)),
            pl.BlockSpec((tm, D_INNER), lambda i: (i, 0)),
            pl.BlockSpec((tm, MLA_HEADS * V_HEAD), lambda i: (i, 0)),
            pl.BlockSpec((tm, d), lambda i: (i, C_GS // D_MODEL)),
            pl.BlockSpec((tm, d), lambda i: (i, C_GM // D_MODEL)),
            full(w["w_br_ssm"]), full(w["w_br_mla"]), full(w["w_out"]),
        ],
        out_specs=pl.BlockSpec((tm, d), lambda i: (i, 0)),
        out_shape=jax.ShapeDtypeStruct((n, d), F32),
        compiler_params=_cparams(("parallel",)),
        name="merge",
    )(x, y_ssm, y_mla, proj, proj, w["w_br_ssm"], w["w_br_mla"], w["w_out"])


def _rope_tables(pos, rows):
    half = QK_ROPE // 2
    inv = ROPE_BASE ** (-jnp.arange(half, dtype=F32) / half)
    ang = pos.astype(F32)[:, None] * inv[None, :]
    cos, sin = jnp.cos(ang), jnp.sin(ang)
    z = lambda w_: jnp.zeros((pos.shape[0], w_), F32)
    cos_t = jnp.concatenate([jnp.ones((pos.shape[0], QK_NOPE), F32), cos, cos, z(32)], axis=1)
    sin_hi = jnp.concatenate([z(QK_NOPE + half), sin, z(32)], axis=1)
    sin_lo = jnp.concatenate([z(QK_NOPE), -sin, z(half + 32)], axis=1)
    reps = max(1, rows // pos.shape[0])
    return tuple(jnp.tile(t, (reps, 1)) for t in (cos_t, sin_hi, sin_lo))


def _layer_weights(i, p):
    row = lambda v: v.reshape(1, -1).astype(F32)
    w_in = p["w_in"][i]
    off_z, off_xbc, off_dt, off_q, off_kv = 2048, 5120, 5152, 5664, 5952
    z = w_in[:, :off_z]
    xbc = w_in[:, off_z:off_xbc]
    dt = w_in[:, off_xbc:off_dt]
    q_lat = w_in[:, off_dt:off_q]
    kv_lat = w_in[:, off_q:off_q + KV_LORA]
    k_rope = w_in[:, off_q + KV_LORA:off_kv]
    gates = w_in[:, off_kv:]
    w_main = jnp.concatenate([z, xbc, gates, q_lat], axis=1).astype(BF16)
    w_main_x = jnp.concatenate([w_main, jnp.zeros((D_MODEL, C_DTX - MAIN_DIM), BF16),
                                jnp.repeat(dt, SSM_HEAD_DIM, axis=1).astype(BF16)], axis=1)
    zeros32 = jnp.zeros((D_MODEL, 32), F32)
    small = jnp.concatenate([kv_lat, dt, zeros32, k_rope, zeros32], axis=1)

    lane_pad = lambda v: jnp.pad(v.astype(F32), (0, LANES - v.shape[0])).reshape(1, LANES)
    a_neg = -jnp.exp(p["a_log"][i].astype(F32))

    w_qb = p["w_qb"][i].reshape(Q_LORA, MLA_HEADS, QK_NOPE + QK_ROPE)
    w_qb_pad = jnp.pad(w_qb, ((0, 0), (0, 0), (0, HEAD_PAD - QK_NOPE - QK_ROPE)))
    w_uk = p["w_uk"][i]
    wk_nope = jnp.pad(w_uk, ((0, 0), (0, 0), (0, HEAD_PAD - QK_NOPE))).reshape(KV_LORA, -1)
    place = jnp.zeros((LANES, MLA_HEADS, HEAD_PAD), F32)
    j = jnp.arange(QK_ROPE)
    place = place.at[S_ROPE - KV_LORA + j, :, QK_NOPE + j].set(1.0).reshape(LANES, -1)
    w_abs = jnp.pad(jnp.transpose(w_uk, (1, 2, 0)), ((0, 0), (0, HEAD_PAD - QK_NOPE), (0, 0)))
    w_uv = p["w_uv"][i]
    eye = jnp.eye(MLA_HEADS, dtype=F32)
    w_uv_bd = (jnp.transpose(w_uv, (1, 0, 2))[:, :, None, :] * eye[:, None, :, None]).reshape(
        MLA_HEADS * KV_LORA, MLA_HEADS * V_HEAD)

    return {
        "ffn1_norm": row(p["ffn1_norm"][i]), "ffn2_norm": row(p["ffn2_norm"][i]),
        "mix_norm": row(p["mix_norm"][i]),
        "w_main": w_main, "w_main_x": w_main_x,
        "w_small": small.astype(BF16),
        "conv_w": p["conv_w"][i].astype(F32), "conv_b": row(p["conv_b"][i]),
        "dt_bias_l": lane_pad(p["dt_bias"][i]), "a_neg_l": lane_pad(a_neg * math.log2(math.e)),
        "dt_bias_x": row(jnp.repeat(p["dt_bias"][i], SSM_HEAD_DIM)),
        "a_neg_x": row(jnp.repeat(a_neg, SSM_HEAD_DIM)),
        "d_skip_x": row(jnp.repeat(p["d_skip"][i], SSM_HEAD_DIM)),
        "ssm_norm": row(p["ssm_norm"][i]),
        "q_norm": row(p["q_norm"][i]), "kv_norm": row(p["kv_norm"][i]),
        "w_qb_pad": w_qb_pad.reshape(Q_LORA, -1).astype(BF16),
        "wk_full": jnp.concatenate([wk_nope, place], axis=0).astype(BF16),
        "w_uv_t": w_uv.reshape(KV_LORA, -1).T.astype(BF16),
        "w_abs": w_abs.astype(BF16),
        "w_uv_bd": w_uv_bd.astype(BF16),
        "ple_norm": row(p["ple_norm"][i]),
    }


def _trunk_layer(x, p_l, tabs, conv_prev, ssm_prev, w, ffn_w, layer, bs, seq, final_norm, final, paged):
    x = ffn_half_step(x, w["ffn1_norm"], ffn_w["ffn1_w_gu"], ffn_w["ffn1_w_down"], layer)
    if paged is None:
        proj, small = in_project(x, w["mix_norm"], w["w_main"], w["w_small"], BF16, (512, 256, 128))
    else:
        proj, small = in_project(x, w["mix_norm"], w["w_main_x"], w["w_small"], F32, (256, 128))
    if paged is None:
        y_ssm, new_conv, new_ssm = ssm_prefill(proj, small, conv_prev, ssm_prev, w, bs, seq)
        c_kv, k_rope, kfull, vt = kv_project(small, w, tabs, True)
        q = q_project(proj, w, tabs, False)
        y_mla = prefill_attention(q, kfull, vt, bs, seq)
    else:
        cache_kv, cache_kr_t, page_table, new_ssm_stack = paged
        y_ssm, new_conv, new_ssm = ssm_step(proj, conv_prev, ssm_prev, new_ssm_stack, layer, w, bs, seq)
        c_kv, k_rope = kv_project(small, w, tabs, False)
        q, qa = q_project(proj, w, tabs, True)
        o_lat = decode_attention(qa, q, c_kv, k_rope, cache_kv, cache_kr_t, page_table, layer, bs, seq)
        y_mla = norm_matmul(o_lat, None, w["w_uv_bd"], BF16, norm=False, tm_prefs=(256, 128))
    x = merge(x, y_ssm, y_mla, proj, ffn_w, layer)
    x = ffn_ple_step(x, p_l, w, ffn_w, layer, final_norm, final)
    return x, c_kv, k_rope, new_conv, new_ssm


def kernel(x_prompt, x_sample, cache_kv, cache_kr, state_conv, state_ssm, page_table, p_prompt, p_sample,
           ffn1_norm, ffn1_w_gu, ffn1_w_down, mix_norm, w_in, conv_w, conv_b, dt_bias, a_log, d_skip,
           ssm_norm, q_norm, w_qb, kv_norm, w_uk, w_uv, w_br_ssm, w_br_mla, w_out,
           ffn2_norm, ffn2_w_gu, ffn2_w_down, ple_norm, w_ple_gate, w_ple_proj, final_norm):
    params = dict(ffn1_norm=ffn1_norm, ffn1_w_gu=ffn1_w_gu, ffn1_w_down=ffn1_w_down, mix_norm=mix_norm,
                  w_in=w_in, conv_w=conv_w, conv_b=conv_b, dt_bias=dt_bias, a_log=a_log, d_skip=d_skip,
                  ssm_norm=ssm_norm, q_norm=q_norm, w_qb=w_qb, kv_norm=kv_norm, w_uk=w_uk, w_uv=w_uv,
                  w_br_ssm=w_br_ssm, w_br_mla=w_br_mla, w_out=w_out, ffn2_norm=ffn2_norm,
                  ffn2_w_gu=ffn2_w_gu, ffn2_w_down=ffn2_w_down, ple_norm=ple_norm,
                  w_ple_gate=w_ple_gate, w_ple_proj=w_ple_proj)
    depth = w_in.shape[0]
    bp, sp, d = x_prompt.shape
    bsm, ss, _ = x_sample.shape
    past_len = page_table.shape[1] * cache_kv.shape[2]
    np_, ns = bp * sp, bsm * ss

    tabs_p = _rope_tables(jnp.arange(sp), sp)
    tabs_s = _rope_tables(past_len + jnp.arange(ss), _pick(ns, (256, 128)))
    conv0 = jnp.zeros((bp, D_CONV - 1, CONV_DIM), F32)
    ssm0 = jnp.zeros((bp, SSM_HEADS, SSM_HEAD_DIM, SSM_STATE), F32)
    fnorm = final_norm.reshape(1, -1).astype(F32)

    cache_kr_t = jnp.swapaxes(cache_kr, 2, 3)
    ffn_w = {name: cast_bf16(params[name])
             for name in ("ffn1_w_gu", "ffn1_w_down", "ffn2_w_gu", "ffn2_w_down", "w_br_ssm",
                          "w_br_mla", "w_out", "w_ple_gate", "w_ple_proj")}

    hp = x_prompt.reshape(np_, d)
    hs = x_sample.reshape(ns, d)
    outs_p, outs_s = [], []
    new_ssm_stack = None
    for i in range(depth):
        last = i == depth - 1
        w = _layer_weights(i, params)
        hs, *rest_s = _trunk_layer(hs, p_sample[i].reshape(ns, PLE_DIM), tabs_s, state_conv,
                                   state_ssm, w, ffn_w, i, bsm, ss, fnorm, last,
                                   (cache_kv, cache_kr_t, page_table, new_ssm_stack))
        hp, *rest_p = _trunk_layer(hp, p_prompt[i].reshape(np_, PLE_DIM), tabs_p, conv0, ssm0, w,
                                   ffn_w, i, bp, sp, fnorm, last, None)
        new_ssm_stack = rest_s[3]
        outs_p.append(rest_p)
        outs_s.append(rest_s)

    stack = lambda outs, k, shape: jnp.stack([o[k] for o in outs]).reshape((depth,) + shape)
    return (
        hp.reshape(bp, sp, d), hs.reshape(bsm, ss, d),
        stack(outs_p, 0, (bp, sp, KV_LORA)), stack(outs_p, 1, (bp, sp, QK_ROPE)),
        stack(outs_p, 2, (bp, D_CONV - 1, CONV_DIM)),
        stack(outs_p, 3, (bp, SSM_HEADS, SSM_HEAD_DIM, SSM_STATE)),
        stack(outs_s, 0, (bsm, ss, KV_LORA)), stack(outs_s, 1, (bsm, ss, QK_ROPE)),
        stack(outs_s, 2, (bsm, D_CONV - 1, CONV_DIM)),
        new_ssm_stack,
    )
```
